```python
import jax, jax.numpy as jnp
from jax import lax
import numpy as np

D_MODEL = 2048
BATCH = 4
SEQ = 4096
DEPTH = 2

GRID_W = 64
CTX_LEN = 256
HEAD_DIM = 128
ROPE_THETA = 10000.0
Q_BLOCK = 128
EPS = 1e-6
NEG_INF = -1e30
A_HEADS = 8
A_KV_HEADS = 2
M_HEADS = 4
M_CHUNK = 128
M_CONV_W = 3
C_HEADS = 4
C_KV_HEADS = 2
WINDOW = 128
N_GROUPS = 4
EXPERTS_PER_GROUP = 4
N_EXPERTS = N_GROUPS * EXPERTS_PER_GROUP
TOP_K = 2
D_FF_EXPERT = 512

A_WIDTH = A_HEADS * HEAD_DIM
A_KV_WIDTH = A_KV_HEADS * HEAD_DIM
M_WIDTH = M_HEADS * HEAD_DIM
C_WIDTH = C_HEADS * HEAD_DIM
C_KV_WIDTH = C_KV_HEADS * HEAD_DIM
SPLIT_WIDTHS = (A_WIDTH, A_KV_WIDTH, A_KV_WIDTH,
                M_WIDTH, M_WIDTH, M_WIDTH, M_WIDTH, 4 * M_HEADS,
                C_WIDTH, C_KV_WIDTH, C_KV_WIDTH,
                3 * D_MODEL)
P_IN = sum(SPLIT_WIDTHS)
SPLIT_IDX = tuple(int(v) for v in np.cumsum(SPLIT_WIDTHS)[:-1])

kernel_name = 'hybrid_gqa_mlstm_swa_hmoe_prefix_dit'

f32 = jnp.float32


def rms_norm(x, g):
    xf = x.astype(f32)
    y = xf * lax.rsqrt(jnp.mean(xf * xf, axis=-1, keepdims=True) + EPS)
    return (y * g.astype(f32)).astype(x.dtype)


def _modulate(x, g, shift, scale):
    return rms_norm(x, g) * (1 + scale) + shift


def _heads(a, n_heads):
    B, T, _ = a.shape
    return a.reshape(B, T, n_heads, HEAD_DIM).transpose(0, 2, 1, 3)


def _merge_heads(a):
    B, H, T, d = a.shape
    return a.transpose(0, 2, 1, 3).reshape(B, T, H * d)


def axial_rope_tables(n_tokens):
    n_rows = n_tokens // GRID_W
    rows, cols = jnp.meshgrid(jnp.arange(n_rows), jnp.arange(GRID_W), indexing='ij')
    rows = rows.reshape(-1).astype(f32)
    cols = cols.reshape(-1).astype(f32)
    axis_dim = HEAD_DIM // 2
    inv_freq = ROPE_THETA ** (-jnp.arange(0, axis_dim, 2, dtype=f32) / axis_dim)
    ang = jnp.concatenate([rows[:, None] * inv_freq, cols[:, None] * inv_freq], axis=-1)
    return jnp.cos(ang), jnp.sin(ang)


def apply_axial_rope(x, cos, sin):
    *lead, T, d = x.shape
    xf = x.astype(f32).reshape(*lead, T, 2, 2, d // 4)
    c = cos.reshape(T, 2, d // 4)
    s = sin.reshape(T, 2, d // 4)
    x1, x2 = xf[..., 0, :], xf[..., 1, :]
    out = jnp.stack([x1 * c - x2 * s, x2 * c + x1 * s], axis=-2)
    return out.reshape(*lead, T, d).astype(x.dtype)


def centred_dwconv(x, w):
    K, C = w.shape
    return lax.conv_general_dilated(x, w.astype(x.dtype)[:, None, :], window_strides=(1,),
                                    padding=[(K // 2, K // 2)],
                                    dimension_numbers=('NWC', 'WIO', 'NWC'),
                                    feature_group_count=C)


def _attend_dense(q, k, v):
    s = jnp.einsum('bkgqd,bksd->bkgqs', q, k, preferred_element_type=f32) * (HEAD_DIM ** -0.5)
    p = jax.nn.softmax(s, axis=-1).astype(v.dtype)
    return jnp.einsum('bkgqs,bksd->bkgqd', p, v)


def global_gqa_latent(q, k_all, v_all):
    B, Hq, T, d = q.shape
    G = Hq // A_KV_HEADS
    nb = T // Q_BLOCK
    qb = q.reshape(B, A_KV_HEADS, G, nb, Q_BLOCK, d).transpose(3, 0, 1, 2, 4, 5)
    ob = lax.map(lambda qq: _attend_dense(qq, k_all, v_all), qb)
    return ob.transpose(1, 2, 3, 0, 4, 5).reshape(B, Hq, T, d)


def global_gqa_context(q, k, v):
    B, Hq, Tc, d = q.shape
    G = Hq // A_KV_HEADS
    o = _attend_dense(q.reshape(B, A_KV_HEADS, G, Tc, d), k, v)
    return o.reshape(B, Hq, Tc, d)


def _sink_probs(s, sink):
    m = jnp.maximum(jnp.max(s, axis=-1, keepdims=True), sink)
    p = jnp.exp(s - m)
    return p / (jnp.sum(p, axis=-1, keepdims=True) + jnp.exp(sink - m))


def window_gqa_latent(q, k, v, k_ctx, v_ctx, sink):
    B, Hq, T, d = q.shape
    Hkv = C_KV_HEADS
    G = Hq // Hkv
    nb = T // Q_BLOCK
    ns = -(-WINDOW // Q_BLOCK)
    nband = 2 * ns + 1
    L = nband * Q_BLOCK

    def band(a):
        ap = jnp.pad(a, ((0, 0), (0, 0), (ns * Q_BLOCK, ns * Q_BLOCK), (0, 0)))
        ap = ap.reshape(B, Hkv, nb + 2 * ns, Q_BLOCK, d)
        return jnp.concatenate([ap[:, :, j:j + nb] for j in range(nband)], axis=3)

    kb, vb = band(k), band(v)
    qb = q.reshape(B, Hkv, G, nb, Q_BLOCK, d)
    scale = HEAD_DIM ** -0.5
    s_loc = jnp.einsum('bkgnqd,bknsd->bkgnqs', qb, kb, preferred_element_type=f32) * scale
    qpos = jnp.arange(nb)[:, None] * Q_BLOCK + jnp.arange(Q_BLOCK)[None, :]
    kpos = jnp.arange(nb)[:, None] * Q_BLOCK + jnp.arange(L)[None, :] - ns * Q_BLOCK
    valid = ((jnp.abs(qpos[:, :, None] - kpos[:, None, :]) <= WINDOW)
             & (kpos[:, None, :] >= 0) & (kpos[:, None, :] < T))
    s_loc = jnp.where(valid, s_loc, NEG_INF)
    s_ctx = jnp.einsum('bkgnqd,bksd->bkgnqs', qb, k_ctx, preferred_element_type=f32) * scale
    s = jnp.concatenate([s_loc, s_ctx], axis=-1)
    p = _sink_probs(s, sink.astype(f32).reshape(1, Hkv, G, 1, 1, 1)).astype(v.dtype)
    o = (jnp.einsum('bkgnqs,bknsd->bkgnqd', p[..., :L], vb)
         + jnp.einsum('bkgnqs,bksd->bkgnqd', p[..., L:], v_ctx))
    return o.reshape(B, Hq, T, d)


def window_gqa_context(q, k, v, sink):
    B, Hq, Tc, d = q.shape
    G = Hq // C_KV_HEADS
    qg = q.reshape(B, C_KV_HEADS, G, Tc, d)
    s = jnp.einsum('bkgqd,bksd->bkgqs', qg, k, preferred_element_type=f32) * (HEAD_DIM ** -0.5)
    p = _sink_probs(s, sink.astype(f32).reshape(1, C_KV_HEADS, G, 1, 1)).astype(v.dtype)
    return jnp.einsum('bkgqs,bksd->bkgqd', p, v).reshape(B, Hq, Tc, d)


def _mlstm_zero_state(B):
    return (jnp.zeros((B, M_HEADS, HEAD_DIM, HEAD_DIM), f32),
            jnp.zeros((B, M_HEADS, HEAD_DIM), f32),
            jnp.full((B, M_HEADS), NEG_INF, f32))


def mlstm_chunk_scan(q, k, v, ig, lf, state):
    B, H, T, d = q.shape
    nc = T // M_CHUNK

    def chunks(a):
        return jnp.moveaxis(a.reshape(B, H, nc, M_CHUNK, *a.shape[3:]), 2, 0)

    tri = jnp.tril(jnp.ones((M_CHUNK, M_CHUNK), bool))

    def step(carry, xs):
        C, n, m = carry
        qc, kc, vc, ic, fc = xs
        b = jnp.cumsum(fc, axis=-1)
        log_d = jnp.where(tri, b[..., :, None] - b[..., None, :] + ic[..., None, :], -jnp.inf)
        log_inter = b + m[..., None]
        m_row = jnp.maximum(log_inter, jnp.max(log_d, axis=-1))
        w_intra = jnp.exp(log_d - m_row[..., None]) * jnp.einsum('bhjd,bhsd->bhjs', qc, kc)
        w_inter = jnp.exp(log_inter - m_row)
        num = (w_inter[..., None] * jnp.einsum('bhjd,bhde->bhje', qc, C)
               + jnp.einsum('bhjs,bhse->bhje', w_intra, vc))
        den = w_inter * jnp.einsum('bhjd,bhd->bhj', qc, n) + jnp.sum(w_intra, axis=-1)
        h = num / jnp.maximum(jnp.abs(den), jnp.exp(-m_row))[..., None]
        b_last = b[..., -1]
        log_w = b_last[..., None] - b + ic
        m_new = jnp.maximum(b_last + m, jnp.max(log_w, axis=-1))
        w_s = jnp.exp(log_w - m_new[..., None])
        decay = jnp.exp(b_last + m - m_new)
        C = decay[..., None, None] * C + jnp.einsum('bhs,bhsd,bhse->bhde', w_s, kc, vc)
        n = decay[..., None] * n + jnp.einsum('bhs,bhsd->bhd', w_s, kc)
        return (C, n, m_new), h

    state, hs = lax.scan(step, state, (chunks(q), chunks(k), chunks(v), chunks(ig), chunks(lf)))
    return jnp.moveaxis(hs, 0, 2).reshape(B, H, T, d), state


def _mlstm_prep(mq, mk, mv, mg, conv_w, ig_b, fg_b):
    qk = jax.nn.silu(centred_dwconv(jnp.concatenate([mq, mk], axis=-1), conv_w))
    q, k = jnp.split(qk, 2, axis=-1)
    q = _heads(q, M_HEADS).astype(f32)
    k = _heads(k, M_HEADS).astype(f32) * (HEAD_DIM ** -0.5)
    v = _heads(mv, M_HEADS).astype(f32)
    B, T, _ = mg.shape
    g = mg.astype(f32).reshape(B, T, 2, 2, M_HEADS).transpose(0, 2, 3, 4, 1)
    ig = g[:, 0] + ig_b.astype(f32)[None, :, :, None]
    lf = jax.nn.log_sigmoid(g[:, 1] + fg_b.astype(f32)[None, :, :, None])
    return q, k, v, ig, lf


def mlstm_bidir(lat, ctx):
    B = lat[0].shape[0]
    h_lat, h_ctx = 0.0, 0.0
    for direction in range(2):
        flip = (lambda a: jnp.flip(a, axis=2)) if direction else (lambda a: a)
        qc, kc, vc, ic, fc = ctx
        hc, st = mlstm_chunk_scan(flip(qc), flip(kc), flip(vc), flip(ic[:, direction]), flip(fc[:, direction]),
                                  _mlstm_zero_state(B))
        ql, kl, vl, il, fl = lat
        hl, _ = mlstm_chunk_scan(flip(ql), flip(kl), flip(vl), flip(il[:, direction]), flip(fl[:, direction]), st)
        h_lat = h_lat + flip(hl)
        h_ctx = h_ctx + flip(hc)
    return h_lat, h_ctx


def _mlstm_out(h, mo, norm_g):
    h = rms_norm(h, norm_g.reshape(M_HEADS, 1, HEAD_DIM))
    return _merge_heads(h).astype(mo.dtype) * jax.nn.sigmoid(mo)


def mixer_sublayer(h_l, h_c, w_in, a_qn_g, a_kn_g, m_conv, m_ig_b, m_fg_b, m_norm_g, c_sink,
                   w_br_a, w_br_m, w_br_c, w_out, need_ctx_out):
    T = h_l.shape[1]
    cos, sin = axial_rope_tables(T)
    (aq_l, ak_l, av_l, mq_l, mk_l, mv_l, mo_l, mg_l, cq_l, ck_l, cv_l, g_l) = jnp.split(h_l @ w_in, SPLIT_IDX, axis=-1)
    (aq_c, ak_c, av_c, mq_c, mk_c, mv_c, mo_c, mg_c, cq_c, ck_c, cv_c, g_c) = jnp.split(h_c @ w_in, SPLIT_IDX, axis=-1)

    aq = apply_axial_rope(rms_norm(_heads(aq_l, A_HEADS), a_qn_g), cos, sin)
    ak = apply_axial_rope(rms_norm(_heads(ak_l, A_KV_HEADS), a_kn_g), cos, sin)
    av = _heads(av_l, A_KV_HEADS)
    akc = rms_norm(_heads(ak_c, A_KV_HEADS), a_kn_g)
    avc = _heads(av_c, A_KV_HEADS)
    oa_l = global_gqa_latent(aq, jnp.concatenate([ak, akc], axis=2), jnp.concatenate([av, avc], axis=2))

    lat_in = _mlstm_prep(mq_l, mk_l, mv_l, mg_l, m_conv, m_ig_b, m_fg_b)
    ctx_in = _mlstm_prep(mq_c, mk_c, mv_c, mg_c, m_conv, m_ig_b, m_fg_b)
    hm_l, hm_c = mlstm_bidir(lat_in, ctx_in)
    om_l = _mlstm_out(hm_l, mo_l, m_norm_g)

    cq = apply_axial_rope(_heads(cq_l, C_HEADS), cos, sin)
    ck = apply_axial_rope(_heads(ck_l, C_KV_HEADS), cos, sin)
    cv = _heads(cv_l, C_KV_HEADS)
    ckc = _heads(ck_c, C_KV_HEADS)
    cvc = _heads(cv_c, C_KV_HEADS)
    oc_l = window_gqa_latent(cq, ck, cv, ckc, cvc, c_sink)

    def merge(oa, om, oc, g):
        ga, gm, gc = jnp.split(jax.nn.sigmoid(g), 3, axis=-1)
        y = ga * (_merge_heads(oa) @ w_br_a) + gm * (om @ w_br_m) + gc * (_merge_heads(oc) @ w_br_c)
        return y @ w_out

    y_l = merge(oa_l, om_l, oc_l, g_l)
    if not need_ctx_out:
        return y_l, None
    oa_c = global_gqa_context(rms_norm(_heads(aq_c, A_HEADS), a_qn_g), akc, avc)
    om_c = _mlstm_out(hm_c, mo_c, m_norm_g)
    oc_c = window_gqa_context(_heads(cq_c, C_HEADS), ckc, cvc, c_sink)
    return y_l, merge(oa_c, om_c, oc_c, g_c)


def hier_moe(h, w_rg, b_rg, w_re, b_re, w_gate, w_up, w_down):
    shape = h.shape
    hf = h.reshape(-1, shape[-1])
    N = hf.shape[0]
    g_logits = (hf @ w_rg).astype(f32) + b_rg.astype(f32)
    g_prob = jax.nn.softmax(g_logits, axis=-1)
    g_sel = jnp.argmax(g_logits, axis=-1)
    p_g = jnp.take_along_axis(g_prob, g_sel[:, None], axis=-1)[:, 0]
    e_logits = ((hf @ w_re).astype(f32) + b_re.astype(f32)).reshape(N, N_GROUPS, EXPERTS_PER_GROUP)
    e_in = jnp.take_along_axis(e_logits, g_sel[:, None, None], axis=1)[:, 0]
    top_p, top_i = lax.top_k(jax.nn.softmax(e_in, axis=-1), TOP_K)
    top_p = top_p / jnp.sum(top_p, axis=-1, keepdims=True)
    expert_ids = g_sel[:, None] * EXPERTS_PER_GROUP + top_i
    weights = p_g[:, None] * top_p
    combine = jnp.sum(jax.nn.one_hot(expert_ids, N_EXPERTS, dtype=f32) * weights[..., None], axis=1)
    a = jnp.einsum('nd,edf->nef', hf, w_gate)
    u = jnp.einsum('nd,edf->nef', hf, w_up)
    mid = jax.nn.silu(a) * u * combine.astype(hf.dtype)[..., None]
    return jnp.einsum('nef,efd->nd', mid, w_down).reshape(shape)


def setup_inputs(seed: int = 0) -> dict:
    key = jax.random.key(seed)
    ks = jax.random.split(key, 32)
    D = D_MODEL

    def nrm(k, shape, scale):
        return jax.random.normal(k, shape, f32) * scale

    return {
        'x': nrm(ks[0], (BATCH, SEQ, D), 1.0),
        'c': nrm(ks[1], (BATCH, D), 1.0),
        'ctx': nrm(ks[2], (BATCH, CTX_LEN, D), 1.0),
        'c_ctx': nrm(ks[3], (D,), 1.0),
        'norm1_g': 1.0 + nrm(ks[4], (DEPTH, D), 0.05),
        'norm2_g': 1.0 + nrm(ks[5], (DEPTH, D), 0.05),
        'w_mod': nrm(ks[6], (DEPTH, D, 6 * D), 0.5 * D ** -0.5),
        'b_mod': nrm(ks[7], (DEPTH, 6 * D), 0.02),
        'w_in': nrm(ks[8], (DEPTH, D, P_IN), D ** -0.5),
        'a_qn_g': 1.0 + nrm(ks[9], (DEPTH, HEAD_DIM), 0.05),
        'a_kn_g': 1.0 + nrm(ks[10], (DEPTH, HEAD_DIM), 0.05),
        'm_conv': nrm(ks[11], (DEPTH, M_CONV_W, 2 * M_WIDTH), M_CONV_W ** -0.5),
        'm_ig_b': nrm(ks[12], (DEPTH, 2, M_HEADS), 0.1),
        'm_fg_b': 3.0 + nrm(ks[13], (DEPTH, 2, M_HEADS), 0.5),
        'm_norm_g': 1.0 + nrm(ks[14], (DEPTH, M_WIDTH), 0.05),
        'c_sink': nrm(ks[15], (DEPTH, C_HEADS), 0.5),
        'w_br_a': nrm(ks[16], (DEPTH, A_WIDTH, D), A_WIDTH ** -0.5),
        'w_br_m': nrm(ks[17], (DEPTH, M_WIDTH, D), M_WIDTH ** -0.5),
        'w_br_c': nrm(ks[18], (DEPTH, C_WIDTH, D), C_WIDTH ** -0.5),
        'w_out': nrm(ks[19], (DEPTH, D, D), D ** -0.5),
        'w_rg': nrm(ks[20], (DEPTH, D, N_GROUPS), D ** -0.5),
        'b_rg': nrm(ks[21], (DEPTH, N_GROUPS), 0.01),
        'w_re': nrm(ks[22], (DEPTH, D, N_EXPERTS), D ** -0.5),
        'b_re': nrm(ks[23], (DEPTH, N_EXPERTS), 0.01),
        'w_gate': nrm(ks[24], (DEPTH, N_EXPERTS, D, D_FF_EXPERT), D ** -0.5),
        'w_up': nrm(ks[25], (DEPTH, N_EXPERTS, D, D_FF_EXPERT), D ** -0.5),
        'w_down': nrm(ks[26], (DEPTH, N_EXPERTS, D_FF_EXPERT, D), D_FF_EXPERT ** -0.5),
        'final_g': 1.0 + nrm(ks[27], (D,), 0.05),
    }


def reference(x, c, ctx, c_ctx, norm1_g, norm2_g, w_mod, b_mod, w_in, a_qn_g, a_kn_g, m_conv, m_ig_b, m_fg_b,
              m_norm_g, c_sink, w_br_a, w_br_m, w_br_c, w_out, w_rg, b_rg, w_re, b_re, w_gate, w_up, w_down,
              final_g):
    s_lat = jax.nn.silu(c)
    s_ctx = jax.nn.silu(c_ctx)
    xl, xc = x, ctx
    for l in range(DEPTH):
        need_ctx_out = l < DEPTH - 1
        mod_l = (s_lat @ w_mod[l] + b_mod[l])[:, None, :]
        mod_c = (s_ctx @ w_mod[l] + b_mod[l])[None, None, :]
        sh1, sc1, g1, sh2, sc2, g2 = jnp.split(mod_l, 6, axis=-1)
        sh1c, sc1c, g1c, sh2c, sc2c, g2c = jnp.split(mod_c, 6, axis=-1)
        hl = _modulate(xl, norm1_g[l], sh1, sc1)
        hc = _modulate(xc, norm1_g[l], sh1c, sc1c)
        y_l, y_c = mixer_sublayer(hl, hc, w_in[l], a_qn_g[l], a_kn_g[l], m_conv[l], m_ig_b[l], m_fg_b[l],
                                  m_norm_g[l], c_sink[l], w_br_a[l], w_br_m[l], w_br_c[l], w_out[l], need_ctx_out)
        xl = xl + g1 * y_l
        xl = xl + g2 * hier_moe(_modulate(xl, norm2_g[l], sh2, sc2), w_rg[l], b_rg[l], w_re[l], b_re[l],
                                w_gate[l], w_up[l], w_down[l])
        if need_ctx_out:
            xc = xc + g1c * y_c
            xc = xc + g2c * hier_moe(_modulate(xc, norm2_g[l], sh2c, sc2c), w_rg[l], b_rg[l], w_re[l], b_re[l],
                                     w_gate[l], w_up[l], w_down[l])
    return rms_norm(xl, final_g)
```

```python
import functools

import jax
import jax.numpy as jnp
from jax import lax
from jax.experimental import pallas as pl
from jax.experimental.pallas import tpu as pltpu

f32 = jnp.float32
bf16 = jnp.bfloat16

HEAD_DIM = 128
GRID_W = 64
ROPE_THETA = 10000.0
EPS = 1e-6
NEG_INF = -1e30
A_HEADS, A_KV_HEADS = 8, 2
M_HEADS, M_CHUNK = 4, 128
C_HEADS, C_KV_HEADS = 4, 2
WINDOW = 128
N_GROUPS, EXPERTS_PER_GROUP = 4, 4
N_EXPERTS = N_GROUPS * EXPERTS_PER_GROUP

A_WIDTH = A_HEADS * HEAD_DIM
A_KV_WIDTH = A_KV_HEADS * HEAD_DIM
M_WIDTH = M_HEADS * HEAD_DIM
C_WIDTH = C_HEADS * HEAD_DIM
C_KV_WIDTH = C_KV_HEADS * HEAD_DIM
N_GATE_COLS = 4 * M_HEADS

LANES = 128
SUBLANES = 8
V7X_VMEM_BYTES = 64 * 1024 * 1024
VMEM_LIMIT = 56 * 1024 * 1024

ROW_TILE = 256
ATTN_SCALE = HEAD_DIM ** -0.5


def _cparams(*sem):
    return pltpu.CompilerParams(dimension_semantics=sem, vmem_limit_bytes=VMEM_LIMIT)


def _pick_tile(n, cap, align):
    best = align
    t = align
    while t <= min(n, cap):
        if n % t == 0:
            best = t
        t += align
    assert n % best == 0
    return best


def _mod_kernel(s_ref, w_ref, b_ref, o_ref):
    s = s_ref[...]
    s = s * jax.nn.sigmoid(s)
    o_ref[...] = jnp.dot(s.astype(bf16), w_ref[...].astype(bf16), preferred_element_type=f32) + b_ref[...]


def _modulation(cond, w_mod, b_mod):
    L, D, N6 = w_mod.shape
    tn = _pick_tile(N6, 1024, LANES)
    return pl.pallas_call(
        _mod_kernel,
        grid=(L, N6 // tn),
        in_specs=[pl.BlockSpec((SUBLANES, D), lambda l, j: (0, 0)),
                  pl.BlockSpec((None, D, tn), lambda l, j: (l, 0, j)),
                  pl.BlockSpec((None, 1, tn), lambda l, j: (l, 0, j))],
        out_specs=pl.BlockSpec((None, SUBLANES, tn), lambda l, j: (l, 0, j)),
        out_shape=jax.ShapeDtypeStruct((L, SUBLANES, N6), f32),
        compiler_params=_cparams("parallel", "parallel"),
        name="modulation",
    )(cond, w_mod, b_mod.reshape(L, 1, N6))


def _rms(x, g):
    return x * lax.rsqrt(jnp.mean(x * x, axis=-1, keepdims=True) + EPS) * g


def _normmod_kernel(x_ref, mod_ref, g_ref, o_ref, *, shift_row, scale_row):
    m = mod_ref[...]
    y = _rms(x_ref[...], g_ref[...])
    o_ref[...] = (y * (1.0 + m[scale_row:scale_row + 1]) + m[shift_row:shift_row + 1]).astype(o_ref.dtype)


def _mod_row(t, tiles_per_batch, n_batch):
    return jnp.where(t % tiles_per_batch == 0, n_batch, t // tiles_per_batch)


def _norm_modulate(xa, mod, g, *, shift_row, scale_row, tpb, n_batch):
    N, D = xa.shape
    return pl.pallas_call(
        functools.partial(_normmod_kernel, shift_row=shift_row, scale_row=scale_row),
        grid=(N // ROW_TILE,),
        in_specs=[pl.BlockSpec((ROW_TILE, D), lambda t: (t, 0)),
                  pl.BlockSpec((None, SUBLANES, D), lambda t: (_mod_row(t, tpb, n_batch), 0, 0)),
                  pl.BlockSpec((1, D), lambda t: (0, 0))],
        out_specs=pl.BlockSpec((ROW_TILE, D), lambda t: (t, 0)),
        out_shape=jax.ShapeDtypeStruct((N, D), bf16),
        compiler_params=_cparams("parallel"),
        name="norm_modulate",
    )(xa, mod, g.reshape(1, D))


def _mm_kernel(a_ref, w_ref, o_ref):
    o_ref[...] = jnp.dot(a_ref[...], w_ref[...], preferred_element_type=f32)


def _matmul(a, w):
    M, K = a.shape
    _, N = w.shape
    tm = _pick_tile(M, 512, ROW_TILE)
    tn = _pick_tile(N, 2304, LANES)
    return pl.pallas_call(
        _mm_kernel,
        grid=(N // tn, M // tm),
        in_specs=[pl.BlockSpec((tm, K), lambda j, i: (i, 0)),
                  pl.BlockSpec((K, tn), lambda j, i: (0, j))],
        out_specs=pl.BlockSpec((tm, tn), lambda j, i: (i, j)),
        out_shape=jax.ShapeDtypeStruct((M, N), f32),
        compiler_params=_cparams("parallel", "parallel"),
        name="in_proj",
    )(a, w)


def _rope(x, cos, sin):
    lane = lax.broadcasted_iota(jnp.int32, x.shape, 1)
    swapped = jnp.where((lane % 64) < 32, pltpu.roll(x, 96, 1), pltpu.roll(x, 32, 1))
    return x * cos + swapped * sin


def _attn_prep_kernel(aq_ref, ak_ref, av_ref, cq_ref, ck_ref, cv_ref, cos_ref, sin_ref, qn_ref, kn_ref,
                      qa_o, ka_o, va_o, qc_o, kc_o, vc_o):
    cos = cos_ref[...]
    sin = sin_ref[...]
    qn = qn_ref[...]
    kn = kn_ref[...]
    for h in range(A_HEADS):
        sl = slice(h * HEAD_DIM, (h + 1) * HEAD_DIM)
        qa_o[:, sl] = _rope(_rms(aq_ref[:, sl], qn), cos, sin).astype(bf16)
    for h in range(A_KV_HEADS):
        sl = slice(h * HEAD_DIM, (h + 1) * HEAD_DIM)
        ka_o[:, sl] = _rope(_rms(ak_ref[:, sl], kn), cos, sin).astype(bf16)
    va_o[...] = av_ref[...].astype(bf16)
    for h in range(C_HEADS):
        sl = slice(h * HEAD_DIM, (h + 1) * HEAD_DIM)
        qc_o[:, sl] = _rope(cq_ref[:, sl], cos, sin).astype(bf16)
    for h in range(C_KV_HEADS):
        sl = slice(h * HEAD_DIM, (h + 1) * HEAD_DIM)
        kc_o[:, sl] = _rope(ck_ref[:, sl], cos, sin).astype(bf16)
    vc_o[...] = cv_ref[...].astype(bf16)


def _attn_prep(P, cols, cos_t, sin_t, qn, kn, tpb):
    N = P.shape[0]

    def pspec(name):
        off, w = cols[name]
        assert off % w == 0
        return pl.BlockSpec((ROW_TILE, w), lambda t, _i=off // w: (t, _i))

    def ospec(w):
        return pl.BlockSpec((ROW_TILE, w), lambda t: (t, 0))

    tab = pl.BlockSpec((ROW_TILE, HEAD_DIM), lambda t: (t % tpb, 0))
    vec = pl.BlockSpec((1, HEAD_DIM), lambda t: (0, 0))
    widths = (A_WIDTH, A_KV_WIDTH, A_KV_WIDTH, C_WIDTH, C_KV_WIDTH, C_KV_WIDTH)
    return pl.pallas_call(
        _attn_prep_kernel,
        grid=(N // ROW_TILE,),
        in_specs=[pspec("aq"), pspec("ak"), pspec("av"), pspec("cq"), pspec("ck"), pspec("cv"),
                  tab, tab, vec, vec],
        out_specs=[ospec(w) for w in widths],
        out_shape=[jax.ShapeDtypeStruct((N, w), bf16) for w in widths],
        compiler_params=_cparams("parallel"),
        name="attn_prep",
    )(P, P, P, P, P, P, cos_t, sin_t, qn.reshape(1, HEAD_DIM), kn.reshape(1, HEAD_DIM))


def _attn_a_kernel(q_ref, k_ref, v_ref, o_ref, m_sc, l_sc, acc_sc, *, q_off, group):
    qi = pl.program_id(2) + q_off
    kj = pl.program_id(3)
    tq = q_ref.shape[0]

    @pl.when(kj == 0)
    def _():
        m_sc[...] = jnp.full(m_sc.shape, NEG_INF, f32)
        l_sc[...] = jnp.zeros(l_sc.shape, f32)
        acc_sc[...] = jnp.zeros(acc_sc.shape, f32)

    @pl.when(jnp.logical_or(qi > 0, kj == 0))
    def _():
        q = q_ref[...]
        q4 = jnp.concatenate([q[:, g * HEAD_DIM:(g + 1) * HEAD_DIM] for g in range(group)], axis=0)
        s = lax.dot_general(q4, k_ref[...], (((1,), (1,)), ((), ())), preferred_element_type=f32) * ATTN_SCALE
        m_prev = m_sc[...]
        m_new = jnp.maximum(m_prev, jnp.max(s, axis=-1, keepdims=True))
        alpha = jnp.exp(m_prev - m_new)
        p = jnp.exp(s - m_new)
        l_sc[...] = alpha * l_sc[...] + jnp.sum(p, axis=-1, keepdims=True)
        acc_sc[...] = alpha * acc_sc[...] + jnp.dot(p.astype(bf16), v_ref[...], preferred_element_type=f32)
        m_sc[...] = m_new

    @pl.when(kj == pl.num_programs(3) - 1)
    def _():
        o = acc_sc[...] / l_sc[...]
        for g in range(group):
            o_ref[:, g * HEAD_DIM:(g + 1) * HEAD_DIM] = o[g * tq:(g + 1) * tq].astype(o_ref.dtype)


def _attn_a(qa, ka, va, n_batch, tpb, need_ctx):
    N = qa.shape[0]
    group = A_HEADS // A_KV_HEADS
    gw = group * HEAD_DIM
    q_off = 0 if need_ctx else 1
    nq = tpb - q_off

    def q_map(b, h, i, j):
        return (b * tpb + i + q_off, h)

    def kv_map(b, h, i, j):
        return (b * tpb + jnp.where(i + q_off > 0, j, 0), h)

    return pl.pallas_call(
        functools.partial(_attn_a_kernel, q_off=q_off, group=group),
        grid=(n_batch, A_KV_HEADS, nq, tpb),
        in_specs=[pl.BlockSpec((ROW_TILE, gw), q_map),
                  pl.BlockSpec((ROW_TILE, HEAD_DIM), kv_map),
                  pl.BlockSpec((ROW_TILE, HEAD_DIM), kv_map)],
        out_specs=pl.BlockSpec((ROW_TILE, gw), q_map),
        out_shape=jax.ShapeDtypeStruct((N, A_WIDTH), bf16),
        scratch_shapes=[pltpu.VMEM((group * ROW_TILE, 1), f32),
                        pltpu.VMEM((group * ROW_TILE, 1), f32),
                        pltpu.VMEM((group * ROW_TILE, HEAD_DIM), f32)],
        compiler_params=_cparams("parallel", "parallel", "parallel", "arbitrary"),
        name="attn_global",
    )(qa, ka, va)


def _attn_c_kernel(sink_ref, q_ref, k_ref, v_ref, o_ref, *, group, n_ctx, n_lat, span):
    kh = pl.program_id(1)
    r = pl.program_id(2)
    tq = q_ref.shape[0]
    q = q_ref[...]
    q2 = jnp.concatenate([q[:, g * HEAD_DIM:(g + 1) * HEAD_DIM] for g in range(group)], axis=0)
    lat_q0 = (r - 1) * tq
    ws = jnp.clip(lat_q0 - WINDOW, 0, n_lat - span)
    start = pl.multiple_of(n_ctx + ws, WINDOW)
    kw = k_ref[pl.ds(start, span), :]
    vw = v_ref[pl.ds(start, span), :]
    kc = k_ref[0:n_ctx, :]
    vc = v_ref[0:n_ctx, :]
    dn = (((1,), (1,)), ((), ()))
    s_loc = lax.dot_general(q2, kw, dn, preferred_element_type=f32) * ATTN_SCALE
    s_ctx = lax.dot_general(q2, kc, dn, preferred_element_type=f32) * ATTN_SCALE
    row = lax.broadcasted_iota(jnp.int32, s_loc.shape, 0)
    col = lax.broadcasted_iota(jnp.int32, s_loc.shape, 1)
    qpos = lat_q0 + row % tq
    kpos = ws + col
    valid = jnp.logical_and(jnp.abs(qpos - kpos) <= WINDOW, r > 0)
    s_loc = jnp.where(valid, s_loc, NEG_INF)
    rowc = lax.broadcasted_iota(jnp.int32, (group * tq, 1), 0)
    sink = jnp.zeros((group * tq, 1), f32)
    for g in range(group):
        sink = jnp.where(rowc // tq == g, sink_ref[kh * group + g], sink)
    m = jnp.maximum(jnp.maximum(jnp.max(s_loc, axis=-1, keepdims=True),
                                jnp.max(s_ctx, axis=-1, keepdims=True)), sink)
    p_loc = jnp.exp(s_loc - m)
    p_ctx = jnp.exp(s_ctx - m)
    den = (jnp.sum(p_loc, axis=-1, keepdims=True) + jnp.sum(p_ctx, axis=-1, keepdims=True)
           + jnp.exp(sink - m))
    o = (jnp.dot(p_loc.astype(bf16), vw, preferred_element_type=f32)
         + jnp.dot(p_ctx.astype(bf16), vc, preferred_element_type=f32)) / den
    for g in range(group):
        o_ref[:, g * HEAD_DIM:(g + 1) * HEAD_DIM] = o[g * tq:(g + 1) * tq].astype(o_ref.dtype)


def _attn_c(qc, kc, vc, sink, n_batch, tpb, n_ctx, n_lat):
    N = qc.shape[0]
    S = n_ctx + n_lat
    group = C_HEADS // C_KV_HEADS
    gw = group * HEAD_DIM
    span = ROW_TILE + 2 * WINDOW
    assert n_lat >= span and n_ctx == ROW_TILE
    kv_spec = pl.BlockSpec((S, HEAD_DIM), lambda b, h, r: (b, h))
    return pl.pallas_call(
        functools.partial(_attn_c_kernel, group=group, n_ctx=n_ctx, n_lat=n_lat, span=span),
        grid=(n_batch, C_KV_HEADS, tpb),
        in_specs=[pl.BlockSpec(memory_space=pltpu.SMEM),
                  pl.BlockSpec((ROW_TILE, gw), lambda b, h, r: (b * tpb + r, h)),
                  kv_spec, kv_spec],
        out_specs=pl.BlockSpec((ROW_TILE, gw), lambda b, h, r: (b * tpb + r, h)),
        out_shape=jax.ShapeDtypeStruct((N, C_WIDTH), bf16),
        compiler_params=_cparams("parallel", "parallel", "parallel"),
        name="attn_window",
    )(sink, qc, kc, vc)


def _chunk_cumsum(x, reverse):
    n = x.shape[0]
    pos = lax.broadcasted_iota(jnp.int32, x.shape, 0) % M_CHUNK
    k = 1
    while k < M_CHUNK:
        if reverse:
            x = x + jnp.where(pos < M_CHUNK - k, pltpu.roll(x, n - k, 0), 0.0)
        else:
            x = x + jnp.where(pos >= k, pltpu.roll(x, k, 0), 0.0)
        k *= 2
    return x


def _mlstm_prep_kernel(q_ref, qp_ref, qn_ref, k_ref, kp_ref, kn_ref, v_ref, g_ref, w_ref, gb_ref,
                       qo, ko, vo, go, *, tpb):
    r = pl.program_id(0) % tpb
    has_prev = r > 1
    has_next = jnp.logical_and(r >= 1, r < tpb - 1)
    n = q_ref.shape[0]
    row = lax.broadcasted_iota(jnp.int32, (n, 1), 0)
    w = w_ref[...]

    def conv_silu(x_ref, p_ref, n_ref, c0):
        x = x_ref[...]
        hp = jnp.where(has_prev, p_ref[SUBLANES - 1:SUBLANES, :], 0.0)
        hn = jnp.where(has_next, n_ref[0:1, :], 0.0)
        xp = jnp.where(row == 0, hp, pltpu.roll(x, 1, 0))
        xn = jnp.where(row == n - 1, hn, pltpu.roll(x, n - 1, 0))
        wc = w[:, c0:c0 + M_WIDTH]
        y = xp * wc[0:1] + x * wc[1:2] + xn * wc[2:3]
        return y * jax.nn.sigmoid(y)

    qo[...] = conv_silu(q_ref, qp_ref, qn_ref, 0).astype(bf16)
    ko[...] = (conv_silu(k_ref, kp_ref, kn_ref, M_WIDTH) * ATTN_SCALE).astype(bf16)
    vo[...] = v_ref[...].astype(bf16)

    g = g_ref[...] + gb_ref[...]
    lane = lax.broadcasted_iota(jnp.int32, g.shape, 1)
    nh2 = 2 * M_HEADS
    lf = jnp.minimum(g, 0.0) - jnp.log1p(jnp.exp(-jnp.abs(g)))
    lf = jnp.where(jnp.logical_and(lane >= nh2, lane < 2 * nh2), lf, 0.0)
    b_fwd = pltpu.roll(_chunk_cumsum(lf, False), nh2, 1)
    b_rev = pltpu.roll(_chunk_cumsum(lf, True), 2 * nh2, 1)
    go[...] = jnp.where(lane < nh2, g, jnp.where(lane < 2 * nh2, lf, jnp.where(lane < 3 * nh2, b_fwd, b_rev)))


def _mlstm_prep(P, cols, m_conv, gate_bias, tpb):
    N = P.shape[0]
    hpt = ROW_TILE // SUBLANES
    nhb = N // SUBLANES

    def main(name):
        off, w = cols[name]
        assert off % w == 0
        return pl.BlockSpec((ROW_TILE, w), lambda t, _i=off // w: (t, _i))

    def prev(name):
        off, w = cols[name]
        return pl.BlockSpec((SUBLANES, w), lambda t, _i=off // w: (jnp.maximum(t * hpt - 1, 0), _i))

    def nxt(name):
        off, w = cols[name]
        return pl.BlockSpec((SUBLANES, w), lambda t, _i=off // w: (jnp.minimum((t + 1) * hpt, nhb - 1), _i))

    ospec = pl.BlockSpec((ROW_TILE, M_WIDTH), lambda t: (t, 0))
    return pl.pallas_call(
        functools.partial(_mlstm_prep_kernel, tpb=tpb),
        grid=(N // ROW_TILE,),
        in_specs=[main("mq"), prev("mq"), nxt("mq"), main("mk"), prev("mk"), nxt("mk"), main("mv"), main("mg"),
                  pl.BlockSpec(m_conv.shape, lambda t: (0, 0)),
                  pl.BlockSpec((1, LANES), lambda t: (0, 0))],
        out_specs=[ospec, ospec, ospec, pl.BlockSpec((ROW_TILE, LANES), lambda t: (t, 0))],
        out_shape=[jax.ShapeDtypeStruct((N, M_WIDTH), bf16)] * 3 + [jax.ShapeDtypeStruct((N, LANES), f32)],
        compiler_params=_cparams("parallel"),
        name="mlstm_prep",
    )(P, P, P, P, P, P, P, P, m_conv, gate_bias)


def _mlstm_scan_kernel(q_ref, k_ref, v_ref, gc_ref, gr_ref, o_ref, c_sc, n_sc, m_sc, *, rev):
    @pl.when(pl.program_id(1) == 0)
    def _():
        c_sc[...] = jnp.zeros(c_sc.shape, f32)
        n_sc[...] = jnp.zeros(n_sc.shape, f32)
        m_sc[...] = jnp.full(m_sc.shape, NEG_INF, f32)

    L = M_CHUNK
    ri = lax.broadcasted_iota(jnp.int32, (L, L), 0)
    ci = lax.broadcasted_iota(jnp.int32, (L, L), 1)
    absorbed = (ci >= ri) if rev else (ci <= ri)
    gc = gc_ref[...]
    gr = gr_ref[...]
    d = 1 if rev else 0
    nt = (((1,), (1,)), ((), ()))
    tn = (((0,), (0,)), ((), ()))
    for h in range(M_HEADS):
        il = d * M_HEADS + h
        bl = (6 if rev else 4) * M_HEADS + d * M_HEADS + h
        i_col, b_col = gc[:, il:il + 1], gc[:, bl:bl + 1]
        i_row, b_row = gr[il:il + 1, :], gr[bl:bl + 1, :]
        sl = slice(h * HEAD_DIM, (h + 1) * HEAD_DIM)
        q, k, v = q_ref[:, sl], k_ref[:, sl], v_ref[:, sl]
        C = c_sc[h]
        nrm = n_sc[h][0:1, :]
        m = m_sc[h][0:1, 0:1]
        b_last = b_col[0:1, :] if rev else b_col[L - 1:L, :]

        log_d = jnp.where(absorbed, b_col - b_row + i_row, -jnp.inf)
        log_inter = b_col + m
        m_row = jnp.maximum(log_inter, jnp.max(log_d, axis=-1, keepdims=True))
        w_intra = jnp.exp(log_d - m_row) * lax.dot_general(q, k, nt, preferred_element_type=f32)
        w_inter = jnp.exp(log_inter - m_row)
        num = (w_inter * jnp.dot(q, C.astype(bf16), preferred_element_type=f32)
               + jnp.dot(w_intra.astype(bf16), v, preferred_element_type=f32))
        den = (w_inter * jnp.sum(q.astype(f32) * nrm, axis=-1, keepdims=True)
               + jnp.sum(w_intra, axis=-1, keepdims=True))
        o_ref[:, sl] = num / jnp.maximum(jnp.abs(den), jnp.exp(-m_row))

        log_w = b_last - b_col + i_col
        m_new = jnp.maximum(b_last + m, jnp.max(log_w, axis=0, keepdims=True))
        kw = jnp.exp(log_w - m_new) * k.astype(f32)
        decay = jnp.exp(b_last + m - m_new)
        c_sc[h] = decay * C + lax.dot_general(kw.astype(bf16), v, tn, preferred_element_type=f32)
        n_sc[h] = jnp.broadcast_to(decay * nrm + jnp.sum(kw, axis=0, keepdims=True), n_sc.shape[1:])
        m_sc[h] = jnp.broadcast_to(m_new, m_sc.shape[1:])


def _mlstm_scan(qm, km, vm, gates, gates_t, n_batch, n_ctx, n_lat, rev):
    N = qm.shape[0]
    ncc, nlc = n_ctx // M_CHUNK, n_lat // M_CHUNK
    cpb = ncc + nlc

    def chunk(s):
        if not rev:
            return s
        return jnp.where(s < ncc, ncc - 1 - s, 2 * ncc + nlc - 1 - s)

    row_spec = pl.BlockSpec((M_CHUNK, M_WIDTH), lambda b, s: (b * cpb + chunk(s), 0))
    return pl.pallas_call(
        functools.partial(_mlstm_scan_kernel, rev=rev),
        grid=(n_batch, cpb),
        in_specs=[row_spec, row_spec, row_spec,
                  pl.BlockSpec((M_CHUNK, LANES), lambda b, s: (b * cpb + chunk(s), 0)),
                  pl.BlockSpec((None, gates_t.shape[1], M_CHUNK), lambda b, s: (b, 0, chunk(s)))],
        out_specs=row_spec,
        out_shape=jax.ShapeDtypeStruct((N, M_WIDTH), f32),
        scratch_shapes=[pltpu.VMEM((M_HEADS, HEAD_DIM, HEAD_DIM), f32),
                        pltpu.VMEM((M_HEADS, SUBLANES, HEAD_DIM), f32),
                        pltpu.VMEM((M_HEADS, SUBLANES, LANES), f32)],
        compiler_params=_cparams("parallel", "arbitrary"),
        name="mlstm_rev" if rev else "mlstm_fwd",
    )(qm, km, vm, gates, gates_t)


def _merge_kernel(oa_ref, oc_ref, hf_ref, hr_ref, mo_ref, g_ref, x_ref, mod_ref, ng_ref,
                  wa_ref, wm_ref, wc_ref, wo_ref, o_ref, *, d_model):
    ng = ng_ref[...]
    mo = mo_ref[...]
    hsum = hf_ref[...] + hr_ref[...]
    parts = []
    for h in range(M_HEADS):
        sl = slice(h * HEAD_DIM, (h + 1) * HEAD_DIM)
        parts.append(_rms(hsum[:, sl], ng[:, sl]))
    om = (jnp.concatenate(parts, axis=-1) * jax.nn.sigmoid(mo)).astype(bf16)
    D = d_model
    ya = jnp.dot(oa_ref[...], wa_ref[...], preferred_element_type=f32)
    y = jax.nn.sigmoid(g_ref[:, 0:D]) * ya
    ym = jnp.dot(om, wm_ref[...], preferred_element_type=f32)
    y = y + jax.nn.sigmoid(g_ref[:, D:2 * D]) * ym
    yc = jnp.dot(oc_ref[...], wc_ref[...], preferred_element_type=f32)
    y = y + jax.nn.sigmoid(g_ref[:, 2 * D:3 * D]) * yc
    out = jnp.dot(y.astype(bf16), wo_ref[...], preferred_element_type=f32)
    o_ref[...] = x_ref[...] + mod_ref[2:3, :] * out


def _merge(oa, oc, hf, hr, P, cols, xa, mod, norm_g, wa, wm, wc, wo, tpb, n_batch):
    N, D = xa.shape
    tm = 128
    sub = ROW_TILE // tm
    mo_off, mo_w = cols["mo"]
    g_off, g_w = cols["g"]
    assert mo_off % mo_w == 0 and g_off == 0

    def rows(w):
        return pl.BlockSpec((tm, w), lambda t: (t, 0))

    def whole(a):
        return pl.BlockSpec(a.shape, lambda t: (0, 0))

    return pl.pallas_call(
        functools.partial(_merge_kernel, d_model=D),
        grid=(N // tm,),
        in_specs=[rows(A_WIDTH), rows(C_WIDTH), rows(M_WIDTH), rows(M_WIDTH),
                  pl.BlockSpec((tm, mo_w), lambda t: (t, mo_off // mo_w)),
                  pl.BlockSpec((tm, g_w), lambda t: (t, 0)),
                  rows(D),
                  pl.BlockSpec((None, SUBLANES, D), lambda t: (_mod_row(t // sub, tpb, n_batch), 0, 0)),
                  pl.BlockSpec((1, M_WIDTH), lambda t: (0, 0)),
                  whole(wa), whole(wm), whole(wc), whole(wo)],
        out_specs=rows(D),
        out_shape=jax.ShapeDtypeStruct((N, D), f32),
        compiler_params=_cparams("parallel"),
        name="merge",
    )(oa, oc, hf, hr, P, P, xa, mod, norm_g.reshape(1, M_WIDTH), wa, wm, wc, wo)


def _router_kernel(x_ref, mod_ref, g_ref, wr_ref, br_ref, h_o, cw_o):
    m = mod_ref[...]
    h = _rms(x_ref[...], g_ref[...]) * (1.0 + m[4:5]) + m[3:4]
    h_o[...] = h.astype(bf16)
    logits = jnp.dot(h, wr_ref[...], preferred_element_type=f32, precision=lax.Precision.HIGHEST) + br_ref[...]
    lane = lax.broadcasted_iota(jnp.int32, logits.shape, 1)
    big = jnp.int32(LANES)
    is_g = jnp.logical_and(lane >= N_EXPERTS, lane < N_EXPERTS + N_GROUPS)
    gl = jnp.where(is_g, logits, -jnp.inf)
    gmax = jnp.max(gl, axis=-1, keepdims=True)
    g_sel = jnp.min(jnp.where(gl == gmax, lane, big), axis=-1, keepdims=True) - N_EXPERTS
    p_g = 1.0 / jnp.sum(jnp.where(is_g, jnp.exp(gl - gmax), 0.0), axis=-1, keepdims=True)
    lo = g_sel * EXPERTS_PER_GROUP
    in_grp = jnp.logical_and(lane >= lo, lane < lo + EXPERTS_PER_GROUP)
    el = jnp.where(in_grp, logits, -jnp.inf)
    e1 = jnp.max(el, axis=-1, keepdims=True)
    i1 = jnp.min(jnp.where(el == e1, lane, big), axis=-1, keepdims=True)
    el2 = jnp.where(lane == i1, -jnp.inf, el)
    e2 = jnp.max(el2, axis=-1, keepdims=True)
    i2 = jnp.min(jnp.where(el2 == e2, lane, big), axis=-1, keepdims=True)
    r = jnp.exp(e2 - e1)
    w1 = p_g / (1.0 + r)
    w2 = p_g * r / (1.0 + r)
    cw_o[...] = jnp.where(lane == i1, w1, 0.0) + jnp.where(lane == i2, w2, 0.0)


def _router(x1, mod, g, w_router, b_router, tpb, n_batch):
    N, D = x1.shape
    return pl.pallas_call(
        _router_kernel,
        grid=(N // ROW_TILE,),
        in_specs=[pl.BlockSpec((ROW_TILE, D), lambda t: (t, 0)),
                  pl.BlockSpec((None, SUBLANES, D), lambda t: (_mod_row(t, tpb, n_batch), 0, 0)),
                  pl.BlockSpec((1, D), lambda t: (0, 0)),
                  pl.BlockSpec((D, LANES), lambda t: (0, 0)),
                  pl.BlockSpec((1, LANES), lambda t: (0, 0))],
        out_specs=[pl.BlockSpec((ROW_TILE, D), lambda t: (t, 0)),
                   pl.BlockSpec((ROW_TILE, LANES), lambda t: (t, 0))],
        out_shape=[jax.ShapeDtypeStruct((N, D), bf16), jax.ShapeDtypeStruct((N, LANES), f32)],
        compiler_params=_cparams("parallel"),
        name="moe_router",
    )(x1, mod, g.reshape(1, D), w_router, b_router)


def _moe_kernel(h_ref, cw_ref, x_ref, *rest, sub):
    mod_refs, (wg_ref, wu_ref, wd_ref, o_ref, acc_sc) = rest[:sub], rest[sub:]
    e = pl.program_id(1)

    @pl.when(e == 0)
    def _():
        acc_sc[...] = jnp.zeros(acc_sc.shape, f32)

    h = h_ref[...]
    cw = cw_ref[...]
    lane = lax.broadcasted_iota(jnp.int32, cw.shape, 1)
    w_e = jnp.sum(jnp.where(lane == e, cw, 0.0), axis=-1, keepdims=True)
    a = jnp.dot(h, wg_ref[...], preferred_element_type=f32)
    u = jnp.dot(h, wu_ref[...], preferred_element_type=f32)
    mid = (a * jax.nn.sigmoid(a)) * u * w_e
    acc_sc[...] += jnp.dot(mid.astype(bf16), wd_ref[...], preferred_element_type=f32)

    @pl.when(e == pl.num_programs(1) - 1)
    def _():
        for i, mod_ref in enumerate(mod_refs):
            sl = slice(i * ROW_TILE, (i + 1) * ROW_TILE)
            o_ref[sl, :] = x_ref[sl, :] + mod_ref[5:6, :] * acc_sc[sl, :]


def _moe_dense(h2, cw, x1, mod, wg, wu, wd, tpb, n_batch):
    N, D = x1.shape
    E, _, F = wg.shape
    tm = _pick_tile(N, 512, ROW_TILE)
    sub = tm // ROW_TILE
    mod_specs = [pl.BlockSpec((None, SUBLANES, D),
                              lambda t, e, _i=i: (_mod_row(t * sub + _i, tpb, n_batch), 0, 0)) for i in range(sub)]
    return pl.pallas_call(
        functools.partial(_moe_kernel, sub=sub),
        grid=(N // tm, E),
        in_specs=[pl.BlockSpec((tm, D), lambda t, e: (t, 0)),
                  pl.BlockSpec((tm, LANES), lambda t, e: (t, 0)),
                  pl.BlockSpec((tm, D), lambda t, e: (t, 0)),
                  *mod_specs,
                  pl.BlockSpec((None, D, F), lambda t, e: (e, 0, 0)),
                  pl.BlockSpec((None, D, F), lambda t, e: (e, 0, 0)),
                  pl.BlockSpec((None, F, D), lambda t, e: (e, 0, 0))],
        out_specs=pl.BlockSpec((tm, D), lambda t, e: (t, 0)),
        out_shape=jax.ShapeDtypeStruct((N, D), f32),
        scratch_shapes=[pltpu.VMEM((tm, D), f32)],
        compiler_params=_cparams("parallel", "arbitrary"),
        name="moe_experts",
    )(h2, cw, x1, *([mod] * sub), wg, wu, wd)


def _final_kernel(x_ref, g_ref, o_ref):
    o_ref[...] = _rms(x_ref[...], g_ref[...])


def _final_norm(xa, g, n_batch, tpb, n_lat):
    N, D = xa.shape
    lt = tpb - 1
    out = pl.pallas_call(
        _final_kernel,
        grid=(n_batch, lt),
        in_specs=[pl.BlockSpec((ROW_TILE, D), lambda b, i: (b * tpb + 1 + i, 0)),
                  pl.BlockSpec((1, D), lambda b, i: (0, 0))],
        out_specs=pl.BlockSpec((ROW_TILE, D), lambda b, i: (b * lt + i, 0)),
        out_shape=jax.ShapeDtypeStruct((n_batch * n_lat, D), f32),
        compiler_params=_cparams("parallel", "parallel"),
        name="final_norm",
    )(xa, g.reshape(1, D))
    return out.reshape(n_batch, n_lat, D)


def _rope_tables(n_ctx, n_lat):
    n_rows = n_lat // GRID_W
    rows, cols = jnp.meshgrid(jnp.arange(n_rows), jnp.arange(GRID_W), indexing='ij')
    rows = rows.reshape(-1).astype(f32)
    cols = cols.reshape(-1).astype(f32)
    axis_dim = HEAD_DIM // 2
    inv_freq = ROPE_THETA ** (-jnp.arange(0, axis_dim, 2, dtype=f32) / axis_dim)
    ar, ac = rows[:, None] * inv_freq, cols[:, None] * inv_freq
    cos = jnp.concatenate([jnp.cos(ar), jnp.cos(ar), jnp.cos(ac), jnp.cos(ac)], axis=-1)
    sin = jnp.concatenate([-jnp.sin(ar), jnp.sin(ar), -jnp.sin(ac), jnp.sin(ac)], axis=-1)
    cos = jnp.concatenate([jnp.ones((n_ctx, HEAD_DIM), f32), cos], axis=0)
    sin = jnp.concatenate([jnp.zeros((n_ctx, HEAD_DIM), f32), sin], axis=0)
    return cos, sin


def _column_plan(d_model):
    names = (("g", 3 * d_model), ("aq", A_WIDTH), ("ak", A_KV_WIDTH), ("av", A_KV_WIDTH),
             ("mq", M_WIDTH), ("mk", M_WIDTH), ("mv", M_WIDTH), ("mo", M_WIDTH),
             ("cq", C_WIDTH), ("ck", C_KV_WIDTH), ("cv", C_KV_WIDTH), ("mg", LANES))
    cols, off = {}, 0
    for name, w in names:
        cols[name] = (off, w)
        off += w
    return cols, off


def _reorder_w_in(w, d_model):
    widths = (A_WIDTH, A_KV_WIDTH, A_KV_WIDTH, M_WIDTH, M_WIDTH, M_WIDTH, M_WIDTH, N_GATE_COLS,
              C_WIDTH, C_KV_WIDTH, C_KV_WIDTH, 3 * d_model)
    names = ("aq", "ak", "av", "mq", "mk", "mv", "mo", "mg", "cq", "ck", "cv", "g")
    pieces, off = {}, 0
    for name, wd in zip(names, widths):
        pieces[name] = w[:, off:off + wd]
        off += wd
    assert off == w.shape[1]
    pieces["mg"] = jnp.pad(pieces["mg"], ((0, 0), (0, LANES - N_GATE_COLS)))
    order = ("g", "aq", "ak", "av", "mq", "mk", "mv", "mo", "cq", "ck", "cv", "mg")
    return jnp.concatenate([pieces[n] for n in order], axis=1).astype(bf16)


def kernel(x, c, ctx, c_ctx, norm1_g, norm2_g, w_mod, b_mod, w_in, a_qn_g, a_kn_g, m_conv, m_ig_b, m_fg_b,
           m_norm_g, c_sink, w_br_a, w_br_m, w_br_c, w_out, w_rg, b_rg, w_re, b_re, w_gate, w_up, w_down,
           final_g):
    B, T, D = x.shape
    Tc = ctx.shape[1]
    L = w_mod.shape[0]
    S = Tc + T
    assert Tc == ROW_TILE and T % ROW_TILE == 0 and B < SUBLANES
    tpb = S // ROW_TILE
    cols, p_width = _column_plan(D)

    xa = jnp.concatenate([ctx, x], axis=1).reshape(B * S, D)
    cond = jnp.zeros((SUBLANES, D), f32).at[:B].set(c).at[B].set(c_ctx)
    mod_all = _modulation(cond, w_mod, b_mod)
    mod_all = mod_all.reshape(L, SUBLANES, 6, D)
    mod_all = jnp.pad(mod_all, ((0, 0), (0, 0), (0, SUBLANES - 6), (0, 0)))
    cos_t, sin_t = _rope_tables(Tc, T)

    for l in range(L):
        need_ctx = l < L - 1
        mod = mod_all[l]
        h1 = _norm_modulate(xa, mod, norm1_g[l], shift_row=0, scale_row=1, tpb=tpb, n_batch=B)
        P = _matmul(h1, _reorder_w_in(w_in[l], D))
        assert P.shape[1] == p_width

        qa, ka, va, qc, kc, vc = _attn_prep(P, cols, cos_t, sin_t, a_qn_g[l], a_kn_g[l], tpb)
        oa = _attn_a(qa, ka, va, B, tpb, need_ctx)
        oc = _attn_c(qc, kc, vc, c_sink[l], B, tpb, Tc, T)

        gate_bias = jnp.zeros((1, LANES), f32)
        gate_bias = gate_bias.at[0, :2 * M_HEADS].set(m_ig_b[l].reshape(-1))
        gate_bias = gate_bias.at[0, 2 * M_HEADS:4 * M_HEADS].set(m_fg_b[l].reshape(-1))
        qm, km, vm, gates = _mlstm_prep(P, cols, m_conv[l], gate_bias, tpb)
        gates_t = gates.reshape(B, S, LANES)[:, :, :8 * M_HEADS].transpose(0, 2, 1)
        hf = _mlstm_scan(qm, km, vm, gates, gates_t, B, Tc, T, rev=False)
        hr = _mlstm_scan(qm, km, vm, gates, gates_t, B, Tc, T, rev=True)

        x1 = _merge(oa, oc, hf, hr, P, cols, xa, mod, m_norm_g[l],
                    w_br_a[l].astype(bf16), w_br_m[l].astype(bf16), w_br_c[l].astype(bf16),
                    w_out[l].astype(bf16), tpb, B)

        w_router = jnp.pad(jnp.concatenate([w_re[l], w_rg[l]], axis=1),
                           ((0, 0), (0, LANES - N_EXPERTS - N_GROUPS)))
        b_router = jnp.pad(jnp.concatenate([b_re[l], b_rg[l]]), (0, LANES - N_EXPERTS - N_GROUPS)).reshape(1, LANES)
        h2, cw = _router(x1, mod, norm2_g[l], w_router, b_router, tpb, B)
        xa = _moe_dense(h2, cw, x1, mod, w_gate[l].astype(bf16), w_up[l].astype(bf16),
                        w_down[l].astype(bf16), tpb, B)

    return _final_norm(xa, final_g, B, tpb, T)
```

```python
import functools

import jax
import jax.numpy as jnp
from jax import lax
from jax.experimental import pallas as pl
from jax.experimental.pallas import tpu as pltpu

f32 = jnp.float32
bf16 = jnp.bfloat16

HEAD_DIM = 128
GRID_W = 64
ROPE_THETA = 10000.0
EPS = 1e-6
NEG_INF = -1e30
A_HEADS, A_KV_HEADS = 8, 2
M_HEADS, M_CHUNK = 4, 128
C_HEADS, C_KV_HEADS = 4, 2
WINDOW = 128
N_GROUPS, EXPERTS_PER_GROUP = 4, 4
N_EXPERTS = N_GROUPS * EXPERTS_PER_GROUP

A_WIDTH = A_HEADS * HEAD_DIM
A_KV_WIDTH = A_KV_HEADS * HEAD_DIM
M_WIDTH = M_HEADS * HEAD_DIM
C_WIDTH = C_HEADS * HEAD_DIM
C_KV_WIDTH = C_KV_HEADS * HEAD_DIM
N_GATE_COLS = 4 * M_HEADS

LANES = 128
SUBLANES = 8
V7X_VMEM_BYTES = 64 * 1024 * 1024
VMEM_LIMIT = 56 * 1024 * 1024

ROW_TILE = 256
ATTN_SCALE = HEAD_DIM ** -0.5


def _cparams(*sem):
    return pltpu.CompilerParams(dimension_semantics=sem, vmem_limit_bytes=VMEM_LIMIT)


def _pick_tile(n, cap, align):
    best = align
    t = align
    while t <= min(n, cap):
        if n % t == 0:
            best = t
        t += align
    assert n % best == 0
    return best


def _mod_kernel(s_ref, w_ref, b_ref, o_ref):
    s = s_ref[...]
    s = s * jax.nn.sigmoid(s)
    o_ref[...] = jnp.dot(s.astype(bf16), w_ref[...].astype(bf16), preferred_element_type=f32) + b_ref[...]


def _modulation(cond, w_mod, b_mod):
    L, D, N6 = w_mod.shape
    tn = _pick_tile(N6, 1024, LANES)
    return pl.pallas_call(
        _mod_kernel,
        grid=(L, N6 // tn),
        in_specs=[pl.BlockSpec((SUBLANES, D), lambda l, j: (0, 0)),
                  pl.BlockSpec((None, D, tn), lambda l, j: (l, 0, j)),
                  pl.BlockSpec((None, 1, tn), lambda l, j: (l, 0, j))],
        out_specs=pl.BlockSpec((None, SUBLANES, tn), lambda l, j: (l, 0, j)),
        out_shape=jax.ShapeDtypeStruct((L, SUBLANES, N6), f32),
        compiler_params=_cparams("parallel", "parallel"),
        name="modulation",
    )(cond, w_mod, b_mod.reshape(L, 1, N6))


def _rms(x, g):
    return x * lax.rsqrt(jnp.mean(x * x, axis=-1, keepdims=True) + EPS) * g


def _normmod_kernel(x_ref, mod_ref, g_ref, o_ref, *, shift_row, scale_row):
    m = mod_ref[...]
    y = _rms(x_ref[...], g_ref[...])
    o_ref[...] = (y * (1.0 + m[scale_row:scale_row + 1]) + m[shift_row:shift_row + 1]).astype(o_ref.dtype)


def _mod_row(t, tiles_per_batch, n_batch):
    return jnp.where(t % tiles_per_batch == 0, n_batch, t // tiles_per_batch)


def _norm_modulate(xa, mod, g, *, shift_row, scale_row, tpb, n_batch):
    N, D = xa.shape
    return pl.pallas_call(
        functools.partial(_normmod_kernel, shift_row=shift_row, scale_row=scale_row),
        grid=(N // ROW_TILE,),
        in_specs=[pl.BlockSpec((ROW_TILE, D), lambda t: (t, 0)),
                  pl.BlockSpec((None, SUBLANES, D), lambda t: (_mod_row(t, tpb, n_batch), 0, 0)),
                  pl.BlockSpec((1, D), lambda t: (0, 0))],
        out_specs=pl.BlockSpec((ROW_TILE, D), lambda t: (t, 0)),
        out_shape=jax.ShapeDtypeStruct((N, D), bf16),
        compiler_params=_cparams("parallel"),
        name="norm_modulate",
    )(xa, mod, g.reshape(1, D))


def _mm_kernel(a_ref, w_ref, o_ref):
    o_ref[...] = jnp.dot(a_ref[...], w_ref[...], preferred_element_type=f32)


def _matmul(a, w):
    M, K = a.shape
    _, N = w.shape
    tm = _pick_tile(M, 512, ROW_TILE)
    tn = _pick_tile(N, 2304, LANES)
    return pl.pallas_call(
        _mm_kernel,
        grid=(N // tn, M // tm),
        in_specs=[pl.BlockSpec((tm, K), lambda j, i: (i, 0)),
                  pl.BlockSpec((K, tn), lambda j, i: (0, j))],
        out_specs=pl.BlockSpec((tm, tn), lambda j, i: (i, j)),
        out_shape=jax.ShapeDtypeStruct((M, N), f32),
        compiler_params=_cparams("parallel", "parallel"),
        name="in_proj",
    )(a, w)


def _rope(x, cos, sin):
    lane = lax.broadcasted_iota(jnp.int32, x.shape, 1)
    swapped = jnp.where((lane % 64) < 32, pltpu.roll(x, 96, 1), pltpu.roll(x, 32, 1))
    return x * cos + swapped * sin


def _attn_prep_kernel(aq_ref, ak_ref, av_ref, cq_ref, ck_ref, cv_ref, cos_ref, sin_ref, qn_ref, kn_ref,
                      qa_o, ka_o, va_o, qc_o, kc_o, vc_o):
    cos = cos_ref[...]
    sin = sin_ref[...]
    qn = qn_ref[...]
    kn = kn_ref[...]
    for h in range(A_HEADS):
        sl = slice(h * HEAD_DIM, (h + 1) * HEAD_DIM)
        qa_o[:, sl] = (_rope(_rms(aq_ref[:, sl], qn), cos, sin) * ATTN_SCALE).astype(bf16)
    for h in range(A_KV_HEADS):
        sl = slice(h * HEAD_DIM, (h + 1) * HEAD_DIM)
        ka_o[:, sl] = _rope(_rms(ak_ref[:, sl], kn), cos, sin).astype(bf16)
    va_o[...] = av_ref[...].astype(bf16)
    for h in range(C_HEADS):
        sl = slice(h * HEAD_DIM, (h + 1) * HEAD_DIM)
        qc_o[:, sl] = _rope(cq_ref[:, sl], cos, sin).astype(bf16)
    for h in range(C_KV_HEADS):
        sl = slice(h * HEAD_DIM, (h + 1) * HEAD_DIM)
        kc_o[:, sl] = _rope(ck_ref[:, sl], cos, sin).astype(bf16)
    vc_o[...] = cv_ref[...].astype(bf16)


def _attn_prep(P, cols, cos_t, sin_t, qn, kn, tpb):
    N = P.shape[0]

    def pspec(name):
        off, w = cols[name]
        assert off % w == 0
        return pl.BlockSpec((ROW_TILE, w), lambda t, _i=off // w: (t, _i))

    def ospec(w):
        return pl.BlockSpec((ROW_TILE, w), lambda t: (t, 0))

    tab = pl.BlockSpec((ROW_TILE, HEAD_DIM), lambda t: (t % tpb, 0))
    vec = pl.BlockSpec((1, HEAD_DIM), lambda t: (0, 0))
    widths = (A_WIDTH, A_KV_WIDTH, A_KV_WIDTH, C_WIDTH, C_KV_WIDTH, C_KV_WIDTH)
    return pl.pallas_call(
        _attn_prep_kernel,
        grid=(N // ROW_TILE,),
        in_specs=[pspec("aq"), pspec("ak"), pspec("av"), pspec("cq"), pspec("ck"), pspec("cv"),
                  tab, tab, vec, vec],
        out_specs=[ospec(w) for w in widths],
        out_shape=[jax.ShapeDtypeStruct((N, w), bf16) for w in widths],
        compiler_params=_cparams("parallel"),
        name="attn_prep",
    )(P, P, P, P, P, P, cos_t, sin_t, qn.reshape(1, HEAD_DIM), kn.reshape(1, HEAD_DIM))


def _attn_a_kernel(q_ref, k_ref, v_ref, o_ref, q_sc, m_sc, l_sc, acc_sc, *, q_off, group, n_ctx, n_lat, tk):
    qi = pl.program_id(2) + q_off
    tq = q_ref.shape[0]
    for g in range(group):
        q_sc[g * tq:(g + 1) * tq, :] = q_ref[:, g * HEAD_DIM:(g + 1) * HEAD_DIM]
    m_sc[...] = jnp.full(m_sc.shape, NEG_INF, f32)
    l_sc[...] = jnp.zeros(l_sc.shape, f32)
    acc_sc[...] = jnp.zeros(acc_sc.shape, f32)

    def chunk(start, size):
        k = k_ref[pl.ds(start, size), :]
        v = v_ref[pl.ds(start, size), :]
        s = lax.dot_general(q_sc[...], k, (((1,), (1,)), ((), ())), preferred_element_type=f32)
        m_prev = m_sc[...]
        m_new = jnp.maximum(m_prev, jnp.max(s, axis=-1, keepdims=True))
        alpha = jnp.exp(m_prev - m_new)
        p = jnp.exp(s - jnp.tile(m_new, (1, size // LANES)))
        psum = p[:, 0:LANES]
        for c in range(1, size // LANES):
            psum = psum + p[:, c * LANES:(c + 1) * LANES]
        l_sc[...] = alpha * l_sc[...] + psum
        acc_sc[...] = alpha * acc_sc[...] + jnp.dot(p.astype(bf16), v, preferred_element_type=f32)
        m_sc[...] = m_new

    chunk(0, n_ctx)

    @pl.when(qi > 0)
    def _():
        def body(j, carry):
            chunk(pl.multiple_of(n_ctx + j * tk, LANES), tk)
            return carry
        lax.fori_loop(0, n_lat // tk, body, 0)

    o = acc_sc[...] / jnp.sum(l_sc[...], axis=-1, keepdims=True)
    for g in range(group):
        o_ref[:, g * HEAD_DIM:(g + 1) * HEAD_DIM] = o[g * tq:(g + 1) * tq].astype(o_ref.dtype)


def _attn_a(qa, ka, va, n_batch, tpb, need_ctx, n_ctx, n_lat):
    N = qa.shape[0]
    S = n_ctx + n_lat
    group = A_HEADS // A_KV_HEADS
    gw = group * HEAD_DIM
    rows = group * ROW_TILE
    q_off = 0 if need_ctx else 1
    tk = _pick_tile(n_lat, 512, LANES)
    assert n_ctx % LANES == 0
    kv_spec = pl.BlockSpec((S, HEAD_DIM), lambda b, h, i: (b, h))
    q_spec = pl.BlockSpec((ROW_TILE, gw), lambda b, h, i: (b * tpb + i + q_off, h))
    return pl.pallas_call(
        functools.partial(_attn_a_kernel, q_off=q_off, group=group, n_ctx=n_ctx, n_lat=n_lat, tk=tk),
        grid=(n_batch, A_KV_HEADS, tpb - q_off),
        in_specs=[q_spec, kv_spec, kv_spec],
        out_specs=q_spec,
        out_shape=jax.ShapeDtypeStruct((N, A_WIDTH), bf16),
        scratch_shapes=[pltpu.VMEM((rows, HEAD_DIM), bf16),
                        pltpu.VMEM((rows, LANES), f32),
                        pltpu.VMEM((rows, LANES), f32),
                        pltpu.VMEM((rows, HEAD_DIM), f32)],
        compiler_params=_cparams("parallel", "parallel", "parallel"),
        name="attn_global",
    )(qa, ka, va)


def _attn_c_kernel(sink_ref, q_ref, k_ref, v_ref, o_ref, *, group, n_ctx, n_lat, span):
    kh = pl.program_id(1)
    r = pl.program_id(2)
    tq = q_ref.shape[0]
    q = q_ref[...]
    q2 = jnp.concatenate([q[:, g * HEAD_DIM:(g + 1) * HEAD_DIM] for g in range(group)], axis=0)
    lat_q0 = (r - 1) * tq
    ws = jnp.clip(lat_q0 - WINDOW, 0, n_lat - span)
    start = pl.multiple_of(n_ctx + ws, WINDOW)
    kw = k_ref[pl.ds(start, span), :]
    vw = v_ref[pl.ds(start, span), :]
    kc = k_ref[0:n_ctx, :]
    vc = v_ref[0:n_ctx, :]
    dn = (((1,), (1,)), ((), ()))
    s_loc = lax.dot_general(q2, kw, dn, preferred_element_type=f32) * ATTN_SCALE
    s_ctx = lax.dot_general(q2, kc, dn, preferred_element_type=f32) * ATTN_SCALE
    row = lax.broadcasted_iota(jnp.int32, s_loc.shape, 0)
    col = lax.broadcasted_iota(jnp.int32, s_loc.shape, 1)
    qpos = lat_q0 + row % tq
    kpos = ws + col
    valid = jnp.logical_and(jnp.abs(qpos - kpos) <= WINDOW, r > 0)
    s_loc = jnp.where(valid, s_loc, NEG_INF)
    rowc = lax.broadcasted_iota(jnp.int32, (group * tq, 1), 0)
    sink = jnp.zeros((group * tq, 1), f32)
    for g in range(group):
        sink = jnp.where(rowc // tq == g, sink_ref[kh * group + g], sink)
    m = jnp.maximum(jnp.maximum(jnp.max(s_loc, axis=-1, keepdims=True),
                                jnp.max(s_ctx, axis=-1, keepdims=True)), sink)
    p_loc = jnp.exp(s_loc - m)
    p_ctx = jnp.exp(s_ctx - m)
    den = (jnp.sum(p_loc, axis=-1, keepdims=True) + jnp.sum(p_ctx, axis=-1, keepdims=True)
           + jnp.exp(sink - m))
    o = (jnp.dot(p_loc.astype(bf16), vw, preferred_element_type=f32)
         + jnp.dot(p_ctx.astype(bf16), vc, preferred_element_type=f32)) / den
    for g in range(group):
        o_ref[:, g * HEAD_DIM:(g + 1) * HEAD_DIM] = o[g * tq:(g + 1) * tq].astype(o_ref.dtype)


def _attn_c(qc, kc, vc, sink, n_batch, tpb, n_ctx, n_lat):
    N = qc.shape[0]
    S = n_ctx + n_lat
    group = C_HEADS // C_KV_HEADS
    gw = group * HEAD_DIM
    span = ROW_TILE + 2 * WINDOW
    assert n_lat >= span and n_ctx == ROW_TILE
    kv_spec = pl.BlockSpec((S, HEAD_DIM), lambda b, h, r: (b, h))
    return pl.pallas_call(
        functools.partial(_attn_c_kernel, group=group, n_ctx=n_ctx, n_lat=n_lat, span=span),
        grid=(n_batch, C_KV_HEADS, tpb),
        in_specs=[pl.BlockSpec(memory_space=pltpu.SMEM),
                  pl.BlockSpec((ROW_TILE, gw), lambda b, h, r: (b * tpb + r, h)),
                  kv_spec, kv_spec],
        out_specs=pl.BlockSpec((ROW_TILE, gw), lambda b, h, r: (b * tpb + r, h)),
        out_shape=jax.ShapeDtypeStruct((N, C_WIDTH), bf16),
        compiler_params=_cparams("parallel", "parallel", "parallel"),
        name="attn_window",
    )(sink, qc, kc, vc)


def _chunk_cumsum(x, reverse):
    n = x.shape[0]
    pos = lax.broadcasted_iota(jnp.int32, x.shape, 0) % M_CHUNK
    k = 1
    while k < M_CHUNK:
        if reverse:
            x = x + jnp.where(pos < M_CHUNK - k, pltpu.roll(x, n - k, 0), 0.0)
        else:
            x = x + jnp.where(pos >= k, pltpu.roll(x, k, 0), 0.0)
        k *= 2
    return x


def _mlstm_prep_kernel(q_ref, qp_ref, qn_ref, k_ref, kp_ref, kn_ref, v_ref, g_ref, w_ref, gb_ref,
                       qo, ko, vo, go, *, tpb):
    r = pl.program_id(0) % tpb
    has_prev = r > 1
    has_next = jnp.logical_and(r >= 1, r < tpb - 1)
    n = q_ref.shape[0]
    row = lax.broadcasted_iota(jnp.int32, (n, 1), 0)
    w = w_ref[...]

    def conv_silu(x_ref, p_ref, n_ref, c0):
        x = x_ref[...]
        hp = jnp.where(has_prev, p_ref[SUBLANES - 1:SUBLANES, :], 0.0)
        hn = jnp.where(has_next, n_ref[0:1, :], 0.0)
        xp = jnp.where(row == 0, hp, pltpu.roll(x, 1, 0))
        xn = jnp.where(row == n - 1, hn, pltpu.roll(x, n - 1, 0))
        wc = w[:, c0:c0 + M_WIDTH]
        y = xp * wc[0:1] + x * wc[1:2] + xn * wc[2:3]
        return y * jax.nn.sigmoid(y)

    qo[...] = conv_silu(q_ref, qp_ref, qn_ref, 0).astype(bf16)
    ko[...] = (conv_silu(k_ref, kp_ref, kn_ref, M_WIDTH) * ATTN_SCALE).astype(bf16)
    vo[...] = v_ref[...].astype(bf16)

    g = g_ref[...] + gb_ref[...]
    lane = lax.broadcasted_iota(jnp.int32, g.shape, 1)
    nh2 = 2 * M_HEADS
    lf = jnp.minimum(g, 0.0) - jnp.log1p(jnp.exp(-jnp.abs(g)))
    lf = jnp.where(jnp.logical_and(lane >= nh2, lane < 2 * nh2), lf, 0.0)
    b_fwd = pltpu.roll(_chunk_cumsum(lf, False), nh2, 1)
    b_rev = pltpu.roll(_chunk_cumsum(lf, True), 2 * nh2, 1)
    go[...] = jnp.where(lane < nh2, g, jnp.where(lane < 2 * nh2, lf, jnp.where(lane < 3 * nh2, b_fwd, b_rev)))


def _mlstm_prep(P, cols, m_conv, gate_bias, tpb):
    N = P.shape[0]
    hpt = ROW_TILE // SUBLANES
    nhb = N // SUBLANES

    def main(name):
        off, w = cols[name]
        assert off % w == 0
        return pl.BlockSpec((ROW_TILE, w), lambda t, _i=off // w: (t, _i))

    def prev(name):
        off, w = cols[name]
        return pl.BlockSpec((SUBLANES, w), lambda t, _i=off // w: (jnp.maximum(t * hpt - 1, 0), _i))

    def nxt(name):
        off, w = cols[name]
        return pl.BlockSpec((SUBLANES, w), lambda t, _i=off // w: (jnp.minimum((t + 1) * hpt, nhb - 1), _i))

    ospec = pl.BlockSpec((ROW_TILE, M_WIDTH), lambda t: (t, 0))
    return pl.pallas_call(
        functools.partial(_mlstm_prep_kernel, tpb=tpb),
        grid=(N // ROW_TILE,),
        in_specs=[main("mq"), prev("mq"), nxt("mq"), main("mk"), prev("mk"), nxt("mk"), main("mv"), main("mg"),
                  pl.BlockSpec(m_conv.shape, lambda t: (0, 0)),
                  pl.BlockSpec((1, LANES), lambda t: (0, 0))],
        out_specs=[ospec, ospec, ospec, pl.BlockSpec((ROW_TILE, LANES), lambda t: (t, 0))],
        out_shape=[jax.ShapeDtypeStruct((N, M_WIDTH), bf16)] * 3 + [jax.ShapeDtypeStruct((N, LANES), f32)],
        compiler_params=_cparams("parallel"),
        name="mlstm_prep",
    )(P, P, P, P, P, P, P, P, m_conv, gate_bias)


def _mlstm_scan_kernel(q_ref, k_ref, v_ref, gc_ref, gr_ref, o_ref, c_sc, n_sc, m_sc, *, rev):
    @pl.when(pl.program_id(1) == 0)
    def _():
        c_sc[...] = jnp.zeros(c_sc.shape, f32)
        n_sc[...] = jnp.zeros(n_sc.shape, f32)
        m_sc[...] = jnp.full(m_sc.shape, NEG_INF, f32)

    L = M_CHUNK
    ri = lax.broadcasted_iota(jnp.int32, (L, L), 0)
    ci = lax.broadcasted_iota(jnp.int32, (L, L), 1)
    absorbed = (ci >= ri) if rev else (ci <= ri)
    gc = gc_ref[...]
    gr = gr_ref[...]
    d = 1 if rev else 0
    nt = (((1,), (1,)), ((), ()))
    tn = (((0,), (0,)), ((), ()))
    for h in range(M_HEADS):
        il = d * M_HEADS + h
        bl = (6 if rev else 4) * M_HEADS + d * M_HEADS + h
        i_col, b_col = gc[:, il:il + 1], gc[:, bl:bl + 1]
        i_row, b_row = gr[il:il + 1, :], gr[bl:bl + 1, :]
        sl = slice(h * HEAD_DIM, (h + 1) * HEAD_DIM)
        q, k, v = q_ref[:, sl], k_ref[:, sl], v_ref[:, sl]
        C = c_sc[h]
        nrm = n_sc[h][0:1, :]
        m = m_sc[h][0:1, 0:1]
        b_last = b_col[0:1, :] if rev else b_col[L - 1:L, :]

        log_d = jnp.where(absorbed, b_col - b_row + i_row, -jnp.inf)
        log_inter = b_col + m
        m_row = jnp.maximum(log_inter, jnp.max(log_d, axis=-1, keepdims=True))
        w_intra = jnp.exp(log_d - m_row) * lax.dot_general(q, k, nt, preferred_element_type=f32)
        w_inter = jnp.exp(log_inter - m_row)
        num = (w_inter * jnp.dot(q, C.astype(bf16), preferred_element_type=f32)
               + jnp.dot(w_intra.astype(bf16), v, preferred_element_type=f32))
        den = (w_inter * jnp.sum(q.astype(f32) * nrm, axis=-1, keepdims=True)
               + jnp.sum(w_intra, axis=-1, keepdims=True))
        o_ref[:, sl] = num / jnp.maximum(jnp.abs(den), jnp.exp(-m_row))

        log_w = b_last - b_col + i_col
        m_new = jnp.maximum(b_last + m, jnp.max(log_w, axis=0, keepdims=True))
        kw = jnp.exp(log_w - m_new) * k.astype(f32)
        decay = jnp.exp(b_last + m - m_new)
        c_sc[h] = decay * C + lax.dot_general(kw.astype(bf16), v, tn, preferred_element_type=f32)
        n_sc[h] = jnp.broadcast_to(decay * nrm + jnp.sum(kw, axis=0, keepdims=True), n_sc.shape[1:])
        m_sc[h] = jnp.broadcast_to(m_new, m_sc.shape[1:])


def _mlstm_scan(qm, km, vm, gates, gates_t, n_batch, n_ctx, n_lat, rev):
    N = qm.shape[0]
    ncc, nlc = n_ctx // M_CHUNK, n_lat // M_CHUNK
    cpb = ncc + nlc

    def chunk(s):
        if not rev:
            return s
        return jnp.where(s < ncc, ncc - 1 - s, 2 * ncc + nlc - 1 - s)

    row_spec = pl.BlockSpec((M_CHUNK, M_WIDTH), lambda b, s: (b * cpb + chunk(s), 0))
    return pl.pallas_call(
        functools.partial(_mlstm_scan_kernel, rev=rev),
        grid=(n_batch, cpb),
        in_specs=[row_spec, row_spec, row_spec,
                  pl.BlockSpec((M_CHUNK, LANES), lambda b, s: (b * cpb + chunk(s), 0)),
                  pl.BlockSpec((None, gates_t.shape[1], M_CHUNK), lambda b, s: (b, 0, chunk(s)))],
        out_specs=row_spec,
        out_shape=jax.ShapeDtypeStruct((N, M_WIDTH), f32),
        scratch_shapes=[pltpu.VMEM((M_HEADS, HEAD_DIM, HEAD_DIM), f32),
                        pltpu.VMEM((M_HEADS, SUBLANES, HEAD_DIM), f32),
                        pltpu.VMEM((M_HEADS, SUBLANES, LANES), f32)],
        compiler_params=_cparams("parallel", "arbitrary"),
        name="mlstm_rev" if rev else "mlstm_fwd",
    )(qm, km, vm, gates, gates_t)


def _merge_kernel(oa_ref, oc_ref, hf_ref, hr_ref, mo_ref, g_ref, x_ref, mod_ref, ng_ref,
                  wa_ref, wm_ref, wc_ref, wo_ref, o_ref, *, d_model):
    ng = ng_ref[...]
    mo = mo_ref[...]
    hsum = hf_ref[...] + hr_ref[...]
    parts = []
    for h in range(M_HEADS):
        sl = slice(h * HEAD_DIM, (h + 1) * HEAD_DIM)
        parts.append(_rms(hsum[:, sl], ng[:, sl]))
    om = (jnp.concatenate(parts, axis=-1) * jax.nn.sigmoid(mo)).astype(bf16)
    D = d_model
    ya = jnp.dot(oa_ref[...], wa_ref[...], preferred_element_type=f32)
    y = jax.nn.sigmoid(g_ref[:, 0:D]) * ya
    ym = jnp.dot(om, wm_ref[...], preferred_element_type=f32)
    y = y + jax.nn.sigmoid(g_ref[:, D:2 * D]) * ym
    yc = jnp.dot(oc_ref[...], wc_ref[...], preferred_element_type=f32)
    y = y + jax.nn.sigmoid(g_ref[:, 2 * D:3 * D]) * yc
    out = jnp.dot(y.astype(bf16), wo_ref[...], preferred_element_type=f32)
    o_ref[...] = x_ref[...] + mod_ref[2:3, :] * out


def _merge(oa, oc, hf, hr, P, cols, xa, mod, norm_g, wa, wm, wc, wo, tpb, n_batch):
    N, D = xa.shape
    tm = 128
    sub = ROW_TILE // tm
    mo_off, mo_w = cols["mo"]
    g_off, g_w = cols["g"]
    assert mo_off % mo_w == 0 and g_off == 0

    def rows(w):
        return pl.BlockSpec((tm, w), lambda t: (t, 0))

    def whole(a):
        return pl.BlockSpec(a.shape, lambda t: (0, 0))

    return pl.pallas_call(
        functools.partial(_merge_kernel, d_model=D),
        grid=(N // tm,),
        in_specs=[rows(A_WIDTH), rows(C_WIDTH), rows(M_WIDTH), rows(M_WIDTH),
                  pl.BlockSpec((tm, mo_w), lambda t: (t, mo_off // mo_w)),
                  pl.BlockSpec((tm, g_w), lambda t: (t, 0)),
                  rows(D),
                  pl.BlockSpec((None, SUBLANES, D), lambda t: (_mod_row(t // sub, tpb, n_batch), 0, 0)),
                  pl.BlockSpec((1, M_WIDTH), lambda t: (0, 0)),
                  whole(wa), whole(wm), whole(wc), whole(wo)],
        out_specs=rows(D),
        out_shape=jax.ShapeDtypeStruct((N, D), f32),
        compiler_params=_cparams("parallel"),
        name="merge",
    )(oa, oc, hf, hr, P, P, xa, mod, norm_g.reshape(1, M_WIDTH), wa, wm, wc, wo)


def _router_kernel(x_ref, mod_ref, g_ref, wr_ref, br_ref, h_o, route_o):
    m = mod_ref[...]
    h = _rms(x_ref[...], g_ref[...]) * (1.0 + m[4:5]) + m[3:4]
    h_o[...] = h
    logits = jnp.dot(h, wr_ref[...], preferred_element_type=f32, precision=lax.Precision.HIGHEST) + br_ref[...]
    lane = lax.broadcasted_iota(jnp.int32, logits.shape, 1)
    big = jnp.int32(LANES)
    is_g = jnp.logical_and(lane >= N_EXPERTS, lane < N_EXPERTS + N_GROUPS)
    gl = jnp.where(is_g, logits, -jnp.inf)
    gmax = jnp.max(gl, axis=-1, keepdims=True)
    g_sel = jnp.min(jnp.where(gl == gmax, lane, big), axis=-1, keepdims=True) - N_EXPERTS
    p_g = 1.0 / jnp.sum(jnp.where(is_g, jnp.exp(gl - gmax), 0.0), axis=-1, keepdims=True)
    lo = g_sel * EXPERTS_PER_GROUP
    in_grp = jnp.logical_and(lane >= lo, lane < lo + EXPERTS_PER_GROUP)
    el = jnp.where(in_grp, logits, -jnp.inf)
    e1 = jnp.max(el, axis=-1, keepdims=True)
    i1 = jnp.min(jnp.where(el == e1, lane, big), axis=-1, keepdims=True)
    el2 = jnp.where(lane == i1, -jnp.inf, el)
    e2 = jnp.max(el2, axis=-1, keepdims=True)
    i2 = jnp.min(jnp.where(el2 == e2, lane, big), axis=-1, keepdims=True)
    r = jnp.exp(e2 - e1)
    w1 = p_g / (1.0 + r)
    w2 = p_g * r / (1.0 + r)
    route_o[...] = jnp.where(lane == 0, i1.astype(f32),
                             jnp.where(lane == 1, i2.astype(f32),
                                       jnp.where(lane == 2, w1, jnp.where(lane == 3, w2, 0.0))))


def _router(x1, mod, g, w_router, b_router, tpb, n_batch):
    N, D = x1.shape
    return pl.pallas_call(
        _router_kernel,
        grid=(N // ROW_TILE,),
        in_specs=[pl.BlockSpec((ROW_TILE, D), lambda t: (t, 0)),
                  pl.BlockSpec((None, SUBLANES, D), lambda t: (_mod_row(t, tpb, n_batch), 0, 0)),
                  pl.BlockSpec((1, D), lambda t: (0, 0)),
                  pl.BlockSpec((D, LANES), lambda t: (0, 0)),
                  pl.BlockSpec((1, LANES), lambda t: (0, 0))],
        out_specs=[pl.BlockSpec((ROW_TILE, D), lambda t: (t, 0)),
                   pl.BlockSpec((ROW_TILE, LANES), lambda t: (t, 0))],
        out_shape=[jax.ShapeDtypeStruct((N, D), f32), jax.ShapeDtypeStruct((N, LANES), f32)],
        compiler_params=_cparams("parallel"),
        name="moe_router",
    )(x1, mod, g.reshape(1, D), w_router, b_router)


MOE_TILE = 256
GATHER_UNROLL = 8


def _dispatch_plan(route):
    N = route.shape[0]
    P = 2 * N
    e_flat = route[:, 0:2].astype(jnp.int32).reshape(P)
    w_flat = route[:, 2:4].reshape(P)
    onehot = (e_flat[:, None] == jnp.arange(N_EXPERTS, dtype=jnp.int32)[None, :]).astype(jnp.int32)
    csum = jnp.cumsum(onehot, axis=0)
    counts = csum[-1]
    rank = jnp.sum((csum - onehot) * onehot, axis=1)
    padded = ((counts + MOE_TILE - 1) // MOE_TILE) * MOE_TILE
    ends = jnp.cumsum(padded)
    starts = ends - padded
    slot = jnp.sum(onehot * starts[None, :], axis=1) + rank
    n_slots = P + N_EXPERTS * MOE_TILE
    tok_of_slot = jnp.zeros((n_slots,), jnp.int32).at[slot].set(jnp.arange(P, dtype=jnp.int32) // 2,
                                                                unique_indices=True)
    w_of_slot = jnp.zeros((n_slots,), f32).at[slot].set(w_flat, unique_indices=True)
    n_tiles = n_slots // MOE_TILE
    tile_start = jnp.arange(n_tiles, dtype=jnp.int32) * MOE_TILE
    tile_expert = jnp.minimum(jnp.sum((tile_start[:, None] >= ends[None, :]).astype(jnp.int32), axis=1),
                              N_EXPERTS - 1)
    n_active = (ends[-1] // MOE_TILE).reshape(1)
    slot_kmajor = slot.reshape(N, 2).T.reshape(P)
    return tile_expert, tok_of_slot, n_active, w_of_slot.reshape(n_slots, 1), slot_kmajor


def _row_copy(src_hbm, row, dst_vmem, dst_row, sem):
    return pltpu.make_async_copy(src_hbm.at[pl.ds(row, 1)], dst_vmem.at[pl.ds(dst_row, 1)], sem)


def _start_row_gather(src_hbm, idx_ref, idx_base, dst_vmem, n_rows, sem):
    def body(r, carry):
        for u in range(GATHER_UNROLL):
            rr = r * GATHER_UNROLL + u
            _row_copy(src_hbm, idx_ref[idx_base + rr], dst_vmem, rr, sem).start()
        return carry
    lax.fori_loop(0, n_rows // GATHER_UNROLL, body, 0)


def _wait_row_gather(src_hbm, dst_vmem, n_rows, sem):
    def body(r, carry):
        for u in range(GATHER_UNROLL):
            _row_copy(src_hbm, 0, dst_vmem, r * GATHER_UNROLL + u, sem).wait()
        return carry
    lax.fori_loop(0, n_rows // GATHER_UNROLL, body, 0)


def _moe_experts_kernel(te_ref, tok_ref, nact_ref, h_hbm, ws_ref, wg_ref, wu_ref, wd_ref, y_ref, xbuf, sem):
    i = pl.program_id(0)
    n_act = nact_ref[0]
    cur = i % 2

    @pl.when(i == 0)
    def _():
        _start_row_gather(h_hbm, tok_ref, 0, xbuf.at[0], MOE_TILE, sem.at[0])

    @pl.when(i + 1 < n_act)
    def _():
        _start_row_gather(h_hbm, tok_ref, (i + 1) * MOE_TILE, xbuf.at[1 - cur], MOE_TILE, sem.at[1 - cur])

    @pl.when(i < n_act)
    def _():
        _wait_row_gather(h_hbm, xbuf.at[cur], MOE_TILE, sem.at[cur])
        x = xbuf[cur].astype(bf16)
        a = jnp.dot(x, wg_ref[...], preferred_element_type=f32)
        u = jnp.dot(x, wu_ref[...], preferred_element_type=f32)
        mid = (a * jax.nn.sigmoid(a)) * u * ws_ref[...]
        y_ref[...] = jnp.dot(mid.astype(bf16), wd_ref[...], preferred_element_type=f32)

    @pl.when(i >= n_act)
    def _():
        y_ref[...] = jnp.zeros(y_ref.shape, f32)


def _moe_experts(h2, plan, wg, wu, wd):
    tile_expert, tok_of_slot, n_active, w_of_slot, _ = plan
    N, D = h2.shape
    E, _, F = wg.shape
    n_slots = w_of_slot.shape[0]
    grid_spec = pltpu.PrefetchScalarGridSpec(
        num_scalar_prefetch=3,
        grid=(n_slots // MOE_TILE,),
        in_specs=[pl.BlockSpec(memory_space=pl.ANY),
                  pl.BlockSpec((MOE_TILE, 1), lambda i, te, tok, na: (i, 0)),
                  pl.BlockSpec((None, D, F), lambda i, te, tok, na: (te[i], 0, 0)),
                  pl.BlockSpec((None, D, F), lambda i, te, tok, na: (te[i], 0, 0)),
                  pl.BlockSpec((None, F, D), lambda i, te, tok, na: (te[i], 0, 0))],
        out_specs=pl.BlockSpec((MOE_TILE, D), lambda i, te, tok, na: (i, 0)),
        scratch_shapes=[pltpu.VMEM((2, MOE_TILE, D), f32), pltpu.SemaphoreType.DMA((2,))])
    return pl.pallas_call(
        _moe_experts_kernel,
        grid_spec=grid_spec,
        out_shape=jax.ShapeDtypeStruct((n_slots, D), f32),
        compiler_params=_cparams("arbitrary"),
        name="moe_experts",
    )(tile_expert, tok_of_slot, n_active, h2, w_of_slot, wg, wu, wd)


def _moe_combine_kernel(slot_ref, y_hbm, x_ref, mod_ref, o_ref, ybuf, sem, *, n_tok):
    i = pl.program_id(0)
    n = pl.num_programs(0)
    cur = i % 2
    tm = x_ref.shape[0]

    def start(tile, buf):
        for k in range(2):
            _start_row_gather(y_hbm, slot_ref, k * n_tok + tile * tm, ybuf.at[buf, k], tm, sem.at[buf])

    @pl.when(i == 0)
    def _():
        start(0, 0)

    @pl.when(i + 1 < n)
    def _():
        start(i + 1, 1 - cur)

    for k in range(2):
        _wait_row_gather(y_hbm, ybuf.at[cur, k], tm, sem.at[cur])
    o_ref[...] = x_ref[...] + mod_ref[5:6, :] * (ybuf[cur, 0] + ybuf[cur, 1])


def _moe_combine(y, plan, x1, mod, tpb, n_batch):
    slot_kmajor = plan[4]
    N, D = x1.shape
    grid_spec = pltpu.PrefetchScalarGridSpec(
        num_scalar_prefetch=1,
        grid=(N // ROW_TILE,),
        in_specs=[pl.BlockSpec(memory_space=pl.ANY),
                  pl.BlockSpec((ROW_TILE, D), lambda t, s: (t, 0)),
                  pl.BlockSpec((None, SUBLANES, D), lambda t, s: (_mod_row(t, tpb, n_batch), 0, 0))],
        out_specs=pl.BlockSpec((ROW_TILE, D), lambda t, s: (t, 0)),
        scratch_shapes=[pltpu.VMEM((2, 2, ROW_TILE, D), f32), pltpu.SemaphoreType.DMA((2,))])
    return pl.pallas_call(
        functools.partial(_moe_combine_kernel, n_tok=N),
        grid_spec=grid_spec,
        out_shape=jax.ShapeDtypeStruct((N, D), f32),
        compiler_params=_cparams("arbitrary"),
        name="moe_combine",
    )(slot_kmajor, y, x1, mod)


def _final_kernel(x_ref, g_ref, o_ref):
    o_ref[...] = _rms(x_ref[...], g_ref[...])


def _final_norm(xa, g, n_batch, tpb, n_lat):
    N, D = xa.shape
    lt = tpb - 1
    out = pl.pallas_call(
        _final_kernel,
        grid=(n_batch, lt),
        in_specs=[pl.BlockSpec((ROW_TILE, D), lambda b, i: (b * tpb + 1 + i, 0)),
                  pl.BlockSpec((1, D), lambda b, i: (0, 0))],
        out_specs=pl.BlockSpec((ROW_TILE, D), lambda b, i: (b * lt + i, 0)),
        out_shape=jax.ShapeDtypeStruct((n_batch * n_lat, D), f32),
        compiler_params=_cparams("parallel", "parallel"),
        name="final_norm",
    )(xa, g.reshape(1, D))
    return out.reshape(n_batch, n_lat, D)


def _rope_tables(n_ctx, n_lat):
    n_rows = n_lat // GRID_W
    rows, cols = jnp.meshgrid(jnp.arange(n_rows), jnp.arange(GRID_W), indexing='ij')
    rows = rows.reshape(-1).astype(f32)
    cols = cols.reshape(-1).astype(f32)
    axis_dim = HEAD_DIM // 2
    inv_freq = ROPE_THETA ** (-jnp.arange(0, axis_dim, 2, dtype=f32) / axis_dim)
    ar, ac = rows[:, None] * inv_freq, cols[:, None] * inv_freq
    cos = jnp.concatenate([jnp.cos(ar), jnp.cos(ar), jnp.cos(ac), jnp.cos(ac)], axis=-1)
    sin = jnp.concatenate([-jnp.sin(ar), jnp.sin(ar), -jnp.sin(ac), jnp.sin(ac)], axis=-1)
    cos = jnp.concatenate([jnp.ones((n_ctx, HEAD_DIM), f32), cos], axis=0)
    sin = jnp.concatenate([jnp.zeros((n_ctx, HEAD_DIM), f32), sin], axis=0)
    return cos, sin


def _column_plan(d_model):
    names = (("g", 3 * d_model), ("aq", A_WIDTH), ("ak", A_KV_WIDTH), ("av", A_KV_WIDTH),
             ("mq", M_WIDTH), ("mk", M_WIDTH), ("mv", M_WIDTH), ("mo", M_WIDTH),
             ("cq", C_WIDTH), ("ck", C_KV_WIDTH), ("cv", C_KV_WIDTH), ("mg", LANES))
    cols, off = {}, 0
    for name, w in names:
        cols[name] = (off, w)
        off += w
    return cols, off


def _reorder_w_in(w, d_model):
    widths = (A_WIDTH, A_KV_WIDTH, A_KV_WIDTH, M_WIDTH, M_WIDTH, M_WIDTH, M_WIDTH, N_GATE_COLS,
              C_WIDTH, C_KV_WIDTH, C_KV_WIDTH, 3 * d_model)
    names = ("aq", "ak", "av", "mq", "mk", "mv", "mo", "mg", "cq", "ck", "cv", "g")
    pieces, off = {}, 0
    for name, wd in zip(names, widths):
        pieces[name] = w[:, off:off + wd]
        off += wd
    assert off == w.shape[1]
    pieces["mg"] = jnp.pad(pieces["mg"], ((0, 0), (0, LANES - N_GATE_COLS)))
    order = ("g", "aq", "ak", "av", "mq", "mk", "mv", "mo", "cq", "ck", "cv", "mg")
    return jnp.concatenate([pieces[n] for n in order], axis=1).astype(bf16)


def kernel(x, c, ctx, c_ctx, norm1_g, norm2_g, w_mod, b_mod, w_in, a_qn_g, a_kn_g, m_conv, m_ig_b, m_fg_b,
           m_norm_g, c_sink, w_br_a, w_br_m, w_br_c, w_out, w_rg, b_rg, w_re, b_re, w_gate, w_up, w_down,
           final_g):
    B, T, D = x.shape
    Tc = ctx.shape[1]
    L = w_mod.shape[0]
    S = Tc + T
    assert Tc == ROW_TILE and T % ROW_TILE == 0 and B < SUBLANES
    tpb = S // ROW_TILE
    cols, p_width = _column_plan(D)

    xa = jnp.concatenate([ctx, x], axis=1).reshape(B * S, D)
    cond = jnp.zeros((SUBLANES, D), f32).at[:B].set(c).at[B].set(c_ctx)
    mod_all = _modulation(cond, w_mod, b_mod)
    mod_all = mod_all.reshape(L, SUBLANES, 6, D)
    mod_all = jnp.pad(mod_all, ((0, 0), (0, 0), (0, SUBLANES - 6), (0, 0)))
    cos_t, sin_t = _rope_tables(Tc, T)

    for l in range(L):
        need_ctx = l < L - 1
        mod = mod_all[l]
        h1 = _norm_modulate(xa, mod, norm1_g[l], shift_row=0, scale_row=1, tpb=tpb, n_batch=B)
        P = _matmul(h1, _reorder_w_in(w_in[l], D))
        assert P.shape[1] == p_width

        qa, ka, va, qc, kc, vc = _attn_prep(P, cols, cos_t, sin_t, a_qn_g[l], a_kn_g[l], tpb)
        oa = _attn_a(qa, ka, va, B, tpb, True, Tc, T)
        oc = _attn_c(qc, kc, vc, c_sink[l], B, tpb, Tc, T)

        gate_bias = jnp.zeros((1, LANES), f32)
        gate_bias = gate_bias.at[0, :2 * M_HEADS].set(m_ig_b[l].reshape(-1))
        gate_bias = gate_bias.at[0, 2 * M_HEADS:4 * M_HEADS].set(m_fg_b[l].reshape(-1))
        qm, km, vm, gates = _mlstm_prep(P, cols, m_conv[l], gate_bias, tpb)
        gates_t = gates.reshape(B, S, LANES)[:, :, :8 * M_HEADS].transpose(0, 2, 1)
        hf = _mlstm_scan(qm, km, vm, gates, gates_t, B, Tc, T, rev=False)
        hr = _mlstm_scan(qm, km, vm, gates, gates_t, B, Tc, T, rev=True)

        x1 = _merge(oa, oc, hf, hr, P, cols, xa, mod, m_norm_g[l],
                    w_br_a[l].astype(bf16), w_br_m[l].astype(bf16), w_br_c[l].astype(bf16),
                    w_out[l].astype(bf16), tpb, B)

        w_router = jnp.pad(jnp.concatenate([w_re[l], w_rg[l]], axis=1),
                           ((0, 0), (0, LANES - N_EXPERTS - N_GROUPS)))
        b_router = jnp.pad(jnp.concatenate([b_re[l], b_rg[l]]), (0, LANES - N_EXPERTS - N_GROUPS)).reshape(1, LANES)
        h2, route = _router(x1, mod, norm2_g[l], w_router, b_router, tpb, B)
        plan = _dispatch_plan(route)
        y = _moe_experts(h2, plan, w_gate[l].astype(bf16), w_up[l].astype(bf16), w_down[l].astype(bf16))
        xa = _moe_combine(y, plan, x1, mod, tpb, B)

    return _final_norm(xa, final_g, B, tpb, T)
```

```python
import functools

import jax
import jax.numpy as jnp
from jax import lax
from jax.experimental import pallas as pl
from jax.experimental.pallas import tpu as pltpu

f32 = jnp.float32
bf16 = jnp.bfloat16

HEAD_DIM = 128
GRID_W = 64
ROPE_THETA = 10000.0
EPS = 1e-6
NEG_INF = -1e30
A_HEADS, A_KV_HEADS = 8, 2
M_HEADS, M_CHUNK = 4, 128
C_HEADS, C_KV_HEADS = 4, 2
WINDOW = 128
N_GROUPS, EXPERTS_PER_GROUP = 4, 4
N_EXPERTS = N_GROUPS * EXPERTS_PER_GROUP

A_WIDTH = A_HEADS * HEAD_DIM
A_KV_WIDTH = A_KV_HEADS * HEAD_DIM
M_WIDTH = M_HEADS * HEAD_DIM
C_WIDTH = C_HEADS * HEAD_DIM
C_KV_WIDTH = C_KV_HEADS * HEAD_DIM
N_GATE_COLS = 4 * M_HEADS

LANES = 128
SUBLANES = 8
V7X_VMEM_BYTES = 64 * 1024 * 1024
VMEM_LIMIT = 56 * 1024 * 1024

ROW_TILE = 256
ATTN_SCALE = HEAD_DIM ** -0.5


def _cparams(*sem):
    return pltpu.CompilerParams(dimension_semantics=sem, vmem_limit_bytes=VMEM_LIMIT)


def _pick_tile(n, cap, align):
    best = align
    t = align
    while t <= min(n, cap):
        if n % t == 0:
            best = t
        t += align
    assert n % best == 0
    return best


def _mod_kernel(s_ref, w_ref, b_ref, o_ref):
    s = s_ref[...]
    s = s * jax.nn.sigmoid(s)
    o_ref[...] = jnp.dot(s.astype(bf16), w_ref[...].astype(bf16), preferred_element_type=f32) + b_ref[...]


def _modulation(cond, w_mod, b_mod):
    L, D, N6 = w_mod.shape
    tn = _pick_tile(N6, 1024, LANES)
    return pl.pallas_call(
        _mod_kernel,
        grid=(L, N6 // tn),
        in_specs=[pl.BlockSpec((SUBLANES, D), lambda l, j: (0, 0)),
                  pl.BlockSpec((None, D, tn), lambda l, j: (l, 0, j)),
                  pl.BlockSpec((None, 1, tn), lambda l, j: (l, 0, j))],
        out_specs=pl.BlockSpec((None, SUBLANES, tn), lambda l, j: (l, 0, j)),
        out_shape=jax.ShapeDtypeStruct((L, SUBLANES, N6), f32),
        compiler_params=_cparams("parallel", "parallel"),
        name="modulation",
    )(cond, w_mod, b_mod.reshape(L, 1, N6))


def _rms(x, g):
    return x * lax.rsqrt(jnp.mean(x * x, axis=-1, keepdims=True) + EPS) * g


def _normmod_kernel(x_ref, mod_ref, g_ref, o_ref, *, shift_row, scale_row):
    m = mod_ref[...]
    y = _rms(x_ref[...], g_ref[...])
    o_ref[...] = (y * (1.0 + m[scale_row:scale_row + 1]) + m[shift_row:shift_row + 1]).astype(o_ref.dtype)


def _mod_row(t, tiles_per_batch, n_batch):
    return jnp.where(t % tiles_per_batch == 0, n_batch, t // tiles_per_batch)


def _norm_modulate(xa, mod, g, *, shift_row, scale_row, tpb, n_batch):
    N, D = xa.shape
    return pl.pallas_call(
        functools.partial(_normmod_kernel, shift_row=shift_row, scale_row=scale_row),
        grid=(N // ROW_TILE,),
        in_specs=[pl.BlockSpec((ROW_TILE, D), lambda t: (t, 0)),
                  pl.BlockSpec((None, SUBLANES, D), lambda t: (_mod_row(t, tpb, n_batch), 0, 0)),
                  pl.BlockSpec((1, D), lambda t: (0, 0))],
        out_specs=pl.BlockSpec((ROW_TILE, D), lambda t: (t, 0)),
        out_shape=jax.ShapeDtypeStruct((N, D), bf16),
        compiler_params=_cparams("parallel"),
        name="norm_modulate",
    )(xa, mod, g.reshape(1, D))


def _mm_kernel(a_ref, w_ref, o_ref):
    o_ref[...] = jnp.dot(a_ref[...], w_ref[...], preferred_element_type=f32)


def _matmul(a, w):
    M, K = a.shape
    _, N = w.shape
    tm = _pick_tile(M, 512, ROW_TILE)
    tn = _pick_tile(N, 2304, LANES)
    return pl.pallas_call(
        _mm_kernel,
        grid=(N // tn, M // tm),
        in_specs=[pl.BlockSpec((tm, K), lambda j, i: (i, 0)),
                  pl.BlockSpec((K, tn), lambda j, i: (0, j))],
        out_specs=pl.BlockSpec((tm, tn), lambda j, i: (i, j)),
        out_shape=jax.ShapeDtypeStruct((M, N), f32),
        compiler_params=_cparams("parallel", "parallel"),
        name="in_proj",
    )(a, w)


def _rope(x, cos, sin):
    lane = lax.broadcasted_iota(jnp.int32, x.shape, 1)
    swapped = jnp.where((lane % 64) < 32, pltpu.roll(x, 96, 1), pltpu.roll(x, 32, 1))
    return x * cos + swapped * sin


def _attn_prep_kernel(aq_ref, ak_ref, av_ref, cq_ref, ck_ref, cv_ref, cos_ref, sin_ref, qn_ref, kn_ref,
                      qa_o, ka_o, va_o, qc_o, kc_o, vc_o):
    cos = cos_ref[...]
    sin = sin_ref[...]
    qn = qn_ref[...]
    kn = kn_ref[...]
    for h in range(A_HEADS):
        sl = slice(h * HEAD_DIM, (h + 1) * HEAD_DIM)
        qa_o[:, sl] = (_rope(_rms(aq_ref[:, sl], qn), cos, sin) * ATTN_SCALE).astype(bf16)
    for h in range(A_KV_HEADS):
        sl = slice(h * HEAD_DIM, (h + 1) * HEAD_DIM)
        ka_o[:, sl] = _rope(_rms(ak_ref[:, sl], kn), cos, sin).astype(bf16)
    va_o[...] = av_ref[...].astype(bf16)
    for h in range(C_HEADS):
        sl = slice(h * HEAD_DIM, (h + 1) * HEAD_DIM)
        qc_o[:, sl] = _rope(cq_ref[:, sl], cos, sin).astype(bf16)
    for h in range(C_KV_HEADS):
        sl = slice(h * HEAD_DIM, (h + 1) * HEAD_DIM)
        kc_o[:, sl] = _rope(ck_ref[:, sl], cos, sin).astype(bf16)
    vc_o[...] = cv_ref[...].astype(bf16)


def _attn_prep(P, cols, cos_t, sin_t, qn, kn, tpb):
    N = P.shape[0]

    def pspec(name):
        off, w = cols[name]
        assert off % w == 0
        return pl.BlockSpec((ROW_TILE, w), lambda t, _i=off // w: (t, _i))

    def ospec(w):
        return pl.BlockSpec((ROW_TILE, w), lambda t: (t, 0))

    tab = pl.BlockSpec((ROW_TILE, HEAD_DIM), lambda t: (t % tpb, 0))
    vec = pl.BlockSpec((1, HEAD_DIM), lambda t: (0, 0))
    widths = (A_WIDTH, A_KV_WIDTH, A_KV_WIDTH, C_WIDTH, C_KV_WIDTH, C_KV_WIDTH)
    return pl.pallas_call(
        _attn_prep_kernel,
        grid=(N // ROW_TILE,),
        in_specs=[pspec("aq"), pspec("ak"), pspec("av"), pspec("cq"), pspec("ck"), pspec("cv"),
                  tab, tab, vec, vec],
        out_specs=[ospec(w) for w in widths],
        out_shape=[jax.ShapeDtypeStruct((N, w), bf16) for w in widths],
        compiler_params=_cparams("parallel"),
        name="attn_prep",
    )(P, P, P, P, P, P, cos_t, sin_t, qn.reshape(1, HEAD_DIM), kn.reshape(1, HEAD_DIM))


def _attn_a_kernel(q_ref, k_ref, v_ref, o_ref, q_sc, m_sc, l_sc, acc_sc, *, q_off, group, n_ctx, n_lat, tk):
    qi = pl.program_id(2) + q_off
    tq = q_ref.shape[0]
    for g in range(group):
        q_sc[g * tq:(g + 1) * tq, :] = q_ref[:, g * HEAD_DIM:(g + 1) * HEAD_DIM]
    m_sc[...] = jnp.full(m_sc.shape, NEG_INF, f32)
    l_sc[...] = jnp.zeros(l_sc.shape, f32)
    acc_sc[...] = jnp.zeros(acc_sc.shape, f32)

    def chunk(start, size):
        k = k_ref[pl.ds(start, size), :]
        v = v_ref[pl.ds(start, size), :]
        s = lax.dot_general(q_sc[...], k, (((1,), (1,)), ((), ())), preferred_element_type=f32)
        m_prev = m_sc[...]
        m_new = jnp.maximum(m_prev, jnp.max(s, axis=-1, keepdims=True))
        alpha = jnp.exp(m_prev - m_new)
        p = jnp.exp(s - jnp.tile(m_new, (1, size // LANES)))
        psum = p[:, 0:LANES]
        for c in range(1, size // LANES):
            psum = psum + p[:, c * LANES:(c + 1) * LANES]
        l_sc[...] = alpha * l_sc[...] + psum
        acc_sc[...] = alpha * acc_sc[...] + jnp.dot(p.astype(bf16), v, preferred_element_type=f32)
        m_sc[...] = m_new

    chunk(0, n_ctx)

    @pl.when(qi > 0)
    def _():
        def body(j, carry):
            chunk(pl.multiple_of(n_ctx + j * tk, LANES), tk)
            return carry
        lax.fori_loop(0, n_lat // tk, body, 0)

    o = acc_sc[...] / jnp.sum(l_sc[...], axis=-1, keepdims=True)
    for g in range(group):
        o_ref[:, g * HEAD_DIM:(g + 1) * HEAD_DIM] = o[g * tq:(g + 1) * tq].astype(o_ref.dtype)


def _attn_a(qa, ka, va, n_batch, tpb, need_ctx, n_ctx, n_lat):
    N = qa.shape[0]
    S = n_ctx + n_lat
    group = A_HEADS // A_KV_HEADS
    gw = group * HEAD_DIM
    rows = group * ROW_TILE
    q_off = 0 if need_ctx else 1
    tk = _pick_tile(n_lat, 1024, LANES)
    assert n_ctx % LANES == 0
    kv_spec = pl.BlockSpec((S, HEAD_DIM), lambda b, h, i: (b, h))
    q_spec = pl.BlockSpec((ROW_TILE, gw), lambda b, h, i: (b * tpb + i + q_off, h))
    return pl.pallas_call(
        functools.partial(_attn_a_kernel, q_off=q_off, group=group, n_ctx=n_ctx, n_lat=n_lat, tk=tk),
        grid=(n_batch, A_KV_HEADS, tpb - q_off),
        in_specs=[q_spec, kv_spec, kv_spec],
        out_specs=q_spec,
        out_shape=jax.ShapeDtypeStruct((N, A_WIDTH), bf16),
        scratch_shapes=[pltpu.VMEM((rows, HEAD_DIM), bf16),
                        pltpu.VMEM((rows, LANES), f32),
                        pltpu.VMEM((rows, LANES), f32),
                        pltpu.VMEM((rows, HEAD_DIM), f32)],
        compiler_params=_cparams("parallel", "parallel", "parallel"),
        name="attn_global",
    )(qa, ka, va)


def _attn_c_kernel(sink_ref, q_ref, k_ref, v_ref, o_ref, *, group, n_ctx, n_lat, span):
    kh = pl.program_id(1)
    r = pl.program_id(2)
    tq = q_ref.shape[0]
    q = q_ref[...]
    q2 = jnp.concatenate([q[:, g * HEAD_DIM:(g + 1) * HEAD_DIM] for g in range(group)], axis=0)
    lat_q0 = (r - 1) * tq
    ws = jnp.clip(lat_q0 - WINDOW, 0, n_lat - span)
    start = pl.multiple_of(n_ctx + ws, WINDOW)
    kw = k_ref[pl.ds(start, span), :]
    vw = v_ref[pl.ds(start, span), :]
    kc = k_ref[0:n_ctx, :]
    vc = v_ref[0:n_ctx, :]
    dn = (((1,), (1,)), ((), ()))
    s_loc = lax.dot_general(q2, kw, dn, preferred_element_type=f32) * ATTN_SCALE
    s_ctx = lax.dot_general(q2, kc, dn, preferred_element_type=f32) * ATTN_SCALE
    row = lax.broadcasted_iota(jnp.int32, s_loc.shape, 0)
    col = lax.broadcasted_iota(jnp.int32, s_loc.shape, 1)
    qpos = lat_q0 + row % tq
    kpos = ws + col
    valid = jnp.logical_and(jnp.abs(qpos - kpos) <= WINDOW, r > 0)
    s_loc = jnp.where(valid, s_loc, NEG_INF)
    rowc = lax.broadcasted_iota(jnp.int32, (group * tq, 1), 0)
    sink = jnp.zeros((group * tq, 1), f32)
    for g in range(group):
        sink = jnp.where(rowc // tq == g, sink_ref[kh * group + g], sink)
    m = jnp.maximum(jnp.maximum(jnp.max(s_loc, axis=-1, keepdims=True),
                                jnp.max(s_ctx, axis=-1, keepdims=True)), sink)
    p_loc = jnp.exp(s_loc - m)
    p_ctx = jnp.exp(s_ctx - m)
    den = (jnp.sum(p_loc, axis=-1, keepdims=True) + jnp.sum(p_ctx, axis=-1, keepdims=True)
           + jnp.exp(sink - m))
    o = (jnp.dot(p_loc.astype(bf16), vw, preferred_element_type=f32)
         + jnp.dot(p_ctx.astype(bf16), vc, preferred_element_type=f32)) / den
    for g in range(group):
        o_ref[:, g * HEAD_DIM:(g + 1) * HEAD_DIM] = o[g * tq:(g + 1) * tq].astype(o_ref.dtype)


def _attn_c(qc, kc, vc, sink, n_batch, tpb, n_ctx, n_lat):
    N = qc.shape[0]
    S = n_ctx + n_lat
    group = C_HEADS // C_KV_HEADS
    gw = group * HEAD_DIM
    span = ROW_TILE + 2 * WINDOW
    assert n_lat >= span and n_ctx == ROW_TILE
    kv_spec = pl.BlockSpec((S, HEAD_DIM), lambda b, h, r: (b, h))
    return pl.pallas_call(
        functools.partial(_attn_c_kernel, group=group, n_ctx=n_ctx, n_lat=n_lat, span=span),
        grid=(n_batch, C_KV_HEADS, tpb),
        in_specs=[pl.BlockSpec(memory_space=pltpu.SMEM),
                  pl.BlockSpec((ROW_TILE, gw), lambda b, h, r: (b * tpb + r, h)),
                  kv_spec, kv_spec],
        out_specs=pl.BlockSpec((ROW_TILE, gw), lambda b, h, r: (b * tpb + r, h)),
        out_shape=jax.ShapeDtypeStruct((N, C_WIDTH), bf16),
        compiler_params=_cparams("parallel", "parallel", "parallel"),
        name="attn_window",
    )(sink, qc, kc, vc)


def _chunk_cumsum(x, reverse):
    n = x.shape[0]
    pos = lax.broadcasted_iota(jnp.int32, x.shape, 0) % M_CHUNK
    k = 1
    while k < M_CHUNK:
        if reverse:
            x = x + jnp.where(pos < M_CHUNK - k, pltpu.roll(x, n - k, 0), 0.0)
        else:
            x = x + jnp.where(pos >= k, pltpu.roll(x, k, 0), 0.0)
        k *= 2
    return x


def _mlstm_prep_kernel(q_ref, qp_ref, qn_ref, k_ref, kp_ref, kn_ref, v_ref, g_ref, w_ref, gb_ref,
                       qo, ko, vo, go, *, tpb):
    r = pl.program_id(0) % tpb
    has_prev = r > 1
    has_next = jnp.logical_and(r >= 1, r < tpb - 1)
    n = q_ref.shape[0]
    row = lax.broadcasted_iota(jnp.int32, (n, 1), 0)
    w = w_ref[...]

    def conv_silu(x_ref, p_ref, n_ref, c0):
        x = x_ref[...]
        hp = jnp.where(has_prev, p_ref[SUBLANES - 1:SUBLANES, :], 0.0)
        hn = jnp.where(has_next, n_ref[0:1, :], 0.0)
        xp = jnp.where(row == 0, hp, pltpu.roll(x, 1, 0))
        xn = jnp.where(row == n - 1, hn, pltpu.roll(x, n - 1, 0))
        wc = w[:, c0:c0 + M_WIDTH]
        y = xp * wc[0:1] + x * wc[1:2] + xn * wc[2:3]
        return y * jax.nn.sigmoid(y)

    qo[...] = conv_silu(q_ref, qp_ref, qn_ref, 0).astype(bf16)
    ko[...] = (conv_silu(k_ref, kp_ref, kn_ref, M_WIDTH) * ATTN_SCALE).astype(bf16)
    vo[...] = v_ref[...].astype(bf16)

    g = g_ref[...] + gb_ref[...]
    lane = lax.broadcasted_iota(jnp.int32, g.shape, 1)
    nh2 = 2 * M_HEADS
    lf = jnp.minimum(g, 0.0) - jnp.log1p(jnp.exp(-jnp.abs(g)))
    lf = jnp.where(jnp.logical_and(lane >= nh2, lane < 2 * nh2), lf, 0.0)
    b_fwd = pltpu.roll(_chunk_cumsum(lf, False), nh2, 1)
    b_rev = pltpu.roll(_chunk_cumsum(lf, True), 2 * nh2, 1)
    go[...] = jnp.where(lane < nh2, g, jnp.where(lane < 2 * nh2, lf, jnp.where(lane < 3 * nh2, b_fwd, b_rev)))


def _mlstm_prep(P, cols, m_conv, gate_bias, tpb):
    N = P.shape[0]
    hpt = ROW_TILE // SUBLANES
    nhb = N // SUBLANES

    def main(name):
        off, w = cols[name]
        assert off % w == 0
        return pl.BlockSpec((ROW_TILE, w), lambda t, _i=off // w: (t, _i))

    def prev(name):
        off, w = cols[name]
        return pl.BlockSpec((SUBLANES, w), lambda t, _i=off // w: (jnp.maximum(t * hpt - 1, 0), _i))

    def nxt(name):
        off, w = cols[name]
        return pl.BlockSpec((SUBLANES, w), lambda t, _i=off // w: (jnp.minimum((t + 1) * hpt, nhb - 1), _i))

    ospec = pl.BlockSpec((ROW_TILE, M_WIDTH), lambda t: (t, 0))
    return pl.pallas_call(
        functools.partial(_mlstm_prep_kernel, tpb=tpb),
        grid=(N // ROW_TILE,),
        in_specs=[main("mq"), prev("mq"), nxt("mq"), main("mk"), prev("mk"), nxt("mk"), main("mv"), main("mg"),
                  pl.BlockSpec(m_conv.shape, lambda t: (0, 0)),
                  pl.BlockSpec((1, LANES), lambda t: (0, 0))],
        out_specs=[ospec, ospec, ospec, pl.BlockSpec((ROW_TILE, LANES), lambda t: (t, 0))],
        out_shape=[jax.ShapeDtypeStruct((N, M_WIDTH), bf16)] * 3 + [jax.ShapeDtypeStruct((N, LANES), f32)],
        compiler_params=_cparams("parallel"),
        name="mlstm_prep",
    )(P, P, P, P, P, P, P, P, m_conv, gate_bias)


def _mlstm_chunk_step(refs, o_ref, c_sc, n_sc, m_sc, rev):
    q_ref, k_ref, v_ref, gc_ref, gr_ref = refs
    L = M_CHUNK
    ri = lax.broadcasted_iota(jnp.int32, (L, L), 0)
    ci = lax.broadcasted_iota(jnp.int32, (L, L), 1)
    absorbed = (ci >= ri) if rev else (ci <= ri)
    gc = gc_ref[...]
    gr = gr_ref[...]
    d = 1 if rev else 0
    nt = (((1,), (1,)), ((), ()))
    tn = (((0,), (0,)), ((), ()))
    for h in range(M_HEADS):
        st = d * M_HEADS + h
        il = d * M_HEADS + h
        bl = (6 if rev else 4) * M_HEADS + d * M_HEADS + h
        i_col, b_col = gc[:, il:il + 1], gc[:, bl:bl + 1]
        i_row, b_row = gr[il:il + 1, :], gr[bl:bl + 1, :]
        sl = slice(h * HEAD_DIM, (h + 1) * HEAD_DIM)
        q, k, v = q_ref[:, sl], k_ref[:, sl], v_ref[:, sl]
        C = c_sc[st]
        nrm = n_sc[st][0:1, :]
        m = m_sc[st][0:1, 0:1]
        b_last = b_col[0:1, :] if rev else b_col[L - 1:L, :]

        log_d = jnp.where(absorbed, b_col - b_row + i_row, -jnp.inf)
        log_inter = b_col + m
        m_row = jnp.maximum(log_inter, jnp.max(log_d, axis=-1, keepdims=True))
        w_intra = jnp.exp(log_d - m_row) * lax.dot_general(q, k, nt, preferred_element_type=f32)
        w_inter = jnp.exp(log_inter - m_row)
        num = (w_inter * jnp.dot(q, C.astype(bf16), preferred_element_type=f32)
               + jnp.dot(w_intra.astype(bf16), v, preferred_element_type=f32))
        den = (w_inter * jnp.sum(q.astype(f32) * nrm, axis=-1, keepdims=True)
               + jnp.sum(w_intra, axis=-1, keepdims=True))
        o_ref[:, sl] = num / jnp.maximum(jnp.abs(den), jnp.exp(-m_row))

        log_w = b_last - b_col + i_col
        m_new = jnp.maximum(b_last + m, jnp.max(log_w, axis=0, keepdims=True))
        kw = jnp.exp(log_w - m_new) * k.astype(f32)
        decay = jnp.exp(b_last + m - m_new)
        c_sc[st] = decay * C + lax.dot_general(kw.astype(bf16), v, tn, preferred_element_type=f32)
        n_sc[st] = jnp.broadcast_to(decay * nrm + jnp.sum(kw, axis=0, keepdims=True), n_sc.shape[1:])
        m_sc[st] = jnp.broadcast_to(m_new, m_sc.shape[1:])


def _mlstm_scan_kernel(*refs):
    fwd_in, rev_in, (of_ref, or_ref, c_sc, n_sc, m_sc) = refs[0:5], refs[5:10], refs[10:]

    @pl.when(pl.program_id(1) == 0)
    def _():
        c_sc[...] = jnp.zeros(c_sc.shape, f32)
        n_sc[...] = jnp.zeros(n_sc.shape, f32)
        m_sc[...] = jnp.full(m_sc.shape, NEG_INF, f32)

    _mlstm_chunk_step(fwd_in, of_ref, c_sc, n_sc, m_sc, False)
    _mlstm_chunk_step(rev_in, or_ref, c_sc, n_sc, m_sc, True)


def _mlstm_scan(qm, km, vm, gates, gates_t, n_batch, n_ctx, n_lat):
    N = qm.shape[0]
    ncc, nlc = n_ctx // M_CHUNK, n_lat // M_CHUNK
    cpb = ncc + nlc

    def rev_chunk(s):
        return jnp.where(s < ncc, ncc - 1 - s, 2 * ncc + nlc - 1 - s)

    def specs(chunk):
        row = pl.BlockSpec((M_CHUNK, M_WIDTH), lambda b, s: (b * cpb + chunk(s), 0))
        return row, [row, row, row,
                     pl.BlockSpec((M_CHUNK, LANES), lambda b, s: (b * cpb + chunk(s), 0)),
                     pl.BlockSpec((None, gates_t.shape[1], M_CHUNK), lambda b, s: (b, 0, chunk(s)))]

    of_spec, fwd_specs = specs(lambda s: s)
    or_spec, rev_specs = specs(rev_chunk)
    args = (qm, km, vm, gates, gates_t)
    return pl.pallas_call(
        _mlstm_scan_kernel,
        grid=(n_batch, cpb),
        in_specs=fwd_specs + rev_specs,
        out_specs=[of_spec, or_spec],
        out_shape=[jax.ShapeDtypeStruct((N, M_WIDTH), f32)] * 2,
        scratch_shapes=[pltpu.VMEM((2 * M_HEADS, HEAD_DIM, HEAD_DIM), f32),
                        pltpu.VMEM((2 * M_HEADS, SUBLANES, HEAD_DIM), f32),
                        pltpu.VMEM((2 * M_HEADS, SUBLANES, LANES), f32)],
        compiler_params=_cparams("parallel", "arbitrary"),
        name="mlstm_scan",
    )(*args, *args)


def _merge_kernel(oa_ref, oc_ref, hf_ref, hr_ref, mo_ref, g_ref, x_ref, mod_ref, ng_ref,
                  wa_ref, wm_ref, wc_ref, wo_ref, o_ref, *, d_model):
    ng = ng_ref[...]
    mo = mo_ref[...]
    hsum = hf_ref[...] + hr_ref[...]
    parts = []
    for h in range(M_HEADS):
        sl = slice(h * HEAD_DIM, (h + 1) * HEAD_DIM)
        parts.append(_rms(hsum[:, sl], ng[:, sl]))
    om = (jnp.concatenate(parts, axis=-1) * jax.nn.sigmoid(mo)).astype(bf16)
    D = d_model
    ya = jnp.dot(oa_ref[...], wa_ref[...], preferred_element_type=f32)
    y = jax.nn.sigmoid(g_ref[:, 0:D]) * ya
    ym = jnp.dot(om, wm_ref[...], preferred_element_type=f32)
    y = y + jax.nn.sigmoid(g_ref[:, D:2 * D]) * ym
    yc = jnp.dot(oc_ref[...], wc_ref[...], preferred_element_type=f32)
    y = y + jax.nn.sigmoid(g_ref[:, 2 * D:3 * D]) * yc
    out = jnp.dot(y.astype(bf16), wo_ref[...], preferred_element_type=f32)
    o_ref[...] = x_ref[...] + mod_ref[2:3, :] * out


def _merge(oa, oc, hf, hr, P, cols, xa, mod, norm_g, wa, wm, wc, wo, tpb, n_batch):
    N, D = xa.shape
    tm = 128
    sub = ROW_TILE // tm
    mo_off, mo_w = cols["mo"]
    g_off, g_w = cols["g"]
    assert mo_off % mo_w == 0 and g_off == 0

    def rows(w):
        return pl.BlockSpec((tm, w), lambda t: (t, 0))

    def whole(a):
        return pl.BlockSpec(a.shape, lambda t: (0, 0))

    return pl.pallas_call(
        functools.partial(_merge_kernel, d_model=D),
        grid=(N // tm,),
        in_specs=[rows(A_WIDTH), rows(C_WIDTH), rows(M_WIDTH), rows(M_WIDTH),
                  pl.BlockSpec((tm, mo_w), lambda t: (t, mo_off // mo_w)),
                  pl.BlockSpec((tm, g_w), lambda t: (t, 0)),
                  rows(D),
                  pl.BlockSpec((None, SUBLANES, D), lambda t: (_mod_row(t // sub, tpb, n_batch), 0, 0)),
                  pl.BlockSpec((1, M_WIDTH), lambda t: (0, 0)),
                  whole(wa), whole(wm), whole(wc), whole(wo)],
        out_specs=rows(D),
        out_shape=jax.ShapeDtypeStruct((N, D), f32),
        compiler_params=_cparams("parallel"),
        name="merge",
    )(oa, oc, hf, hr, P, P, xa, mod, norm_g.reshape(1, M_WIDTH), wa, wm, wc, wo)


def _router_kernel(x_ref, mod_ref, g_ref, wr_ref, br_ref, route_o):
    m = mod_ref[...]
    h = _rms(x_ref[...], g_ref[...]) * (1.0 + m[4:5]) + m[3:4]
    logits = jnp.dot(h, wr_ref[...], preferred_element_type=f32, precision=lax.Precision.HIGHEST) + br_ref[...]
    lane = lax.broadcasted_iota(jnp.int32, logits.shape, 1)
    big = jnp.int32(LANES)
    is_g = jnp.logical_and(lane >= N_EXPERTS, lane < N_EXPERTS + N_GROUPS)
    gl = jnp.where(is_g, logits, -jnp.inf)
    gmax = jnp.max(gl, axis=-1, keepdims=True)
    g_sel = jnp.min(jnp.where(gl == gmax, lane, big), axis=-1, keepdims=True) - N_EXPERTS
    p_g = 1.0 / jnp.sum(jnp.where(is_g, jnp.exp(gl - gmax), 0.0), axis=-1, keepdims=True)
    lo = g_sel * EXPERTS_PER_GROUP
    in_grp = jnp.logical_and(lane >= lo, lane < lo + EXPERTS_PER_GROUP)
    el = jnp.where(in_grp, logits, -jnp.inf)
    e1 = jnp.max(el, axis=-1, keepdims=True)
    i1 = jnp.min(jnp.where(el == e1, lane, big), axis=-1, keepdims=True)
    el2 = jnp.where(lane == i1, -jnp.inf, el)
    e2 = jnp.max(el2, axis=-1, keepdims=True)
    i2 = jnp.min(jnp.where(el2 == e2, lane, big), axis=-1, keepdims=True)
    r = jnp.exp(e2 - e1)
    w1 = p_g / (1.0 + r)
    w2 = p_g * r / (1.0 + r)
    route_o[...] = jnp.where(lane == 0, i1.astype(f32),
                             jnp.where(lane == 1, i2.astype(f32),
                                       jnp.where(lane == 2, w1, jnp.where(lane == 3, w2, 0.0))))


def _router(x1, mod, g, w_router, b_router, tpb, n_batch):
    N, D = x1.shape
    return pl.pallas_call(
        _router_kernel,
        grid=(N // ROW_TILE,),
        in_specs=[pl.BlockSpec((ROW_TILE, D), lambda t: (t, 0)),
                  pl.BlockSpec((None, SUBLANES, D), lambda t: (_mod_row(t, tpb, n_batch), 0, 0)),
                  pl.BlockSpec((1, D), lambda t: (0, 0)),
                  pl.BlockSpec((D, LANES), lambda t: (0, 0)),
                  pl.BlockSpec((1, LANES), lambda t: (0, 0))],
        out_specs=pl.BlockSpec((ROW_TILE, LANES), lambda t: (t, 0)),
        out_shape=jax.ShapeDtypeStruct((N, LANES), f32),
        compiler_params=_cparams("parallel"),
        name="moe_router",
    )(x1, mod, g.reshape(1, D), w_router, b_router)


MOE_TILE = 256
DMA_UNROLL = 8


def _dispatch_plan(route):
    N = route.shape[0]
    P = 2 * N
    e_flat = route[:, 0:2].astype(jnp.int32).reshape(P)
    lanes = jnp.arange(N_EXPERTS, dtype=jnp.int32)
    onehot = (e_flat[:, None] == lanes[None, :]).astype(jnp.int32)
    csum = jnp.cumsum(onehot, axis=0)
    counts = csum[-1]
    rank = jnp.sum((csum - onehot) * onehot, axis=1)
    padded = ((counts + MOE_TILE - 1) // MOE_TILE) * MOE_TILE
    ends = jnp.cumsum(padded)
    starts = ends - padded
    slot = jnp.sum(onehot * starts[None, :], axis=1) + rank
    n_slots = P + N_EXPERTS * MOE_TILE
    n_tiles = n_slots // MOE_TILE
    tile_start = jnp.arange(n_tiles, dtype=jnp.int32) * MOE_TILE
    tile_expert = jnp.minimum(jnp.sum((tile_start[:, None] >= ends[None, :]).astype(jnp.int32), axis=1),
                              N_EXPERTS - 1)
    n_active = (ends[-1] // MOE_TILE).reshape(1)
    n_pad_e = padded - counts
    pad_off = jnp.cumsum(n_pad_e) - n_pad_e
    n_pad = jnp.sum(n_pad_e)
    q = jnp.arange(n_slots - P, dtype=jnp.int32)
    e_of_q = jnp.minimum(jnp.sum((q[:, None] >= (pad_off + n_pad_e)[None, :]).astype(jnp.int32), axis=1),
                         N_EXPERTS - 1)
    oh_q = (e_of_q[:, None] == lanes[None, :]).astype(jnp.int32)
    in_run = jnp.sum(oh_q * (starts + counts - pad_off)[None, :], axis=1) + q
    pad_slots = jnp.where(q < n_pad, in_run, ends[-1] + q - n_pad)
    slot_kmajor = slot.reshape(N, 2).T.reshape(P)
    return dict(tile_expert=tile_expert, n_active=n_active, slot_kmajor=slot_kmajor, pad_slots=pad_slots,
                n_slots=n_slots)


def _row_copy(src, src_row, dst, dst_row, sem):
    return pltpu.make_async_copy(src.at[pl.ds(src_row, 1)], dst.at[pl.ds(dst_row, 1)], sem)


def _wait_row_copies(src, dst, n_rows, sem):
    def body(r, carry):
        for u in range(DMA_UNROLL):
            _row_copy(src, 0, dst, 0, sem).wait()
        return carry
    lax.fori_loop(0, n_rows // DMA_UNROLL, body, 0)


def _moe_dispatch_kernel(slot_ref, pad_ref, x_ref, mod_ref, g_ref, xs_hbm, hbuf, zrow, sem, zsem, *, n_tok):
    i = pl.program_id(0)
    n = pl.num_programs(0)
    cur = i % 2
    tm = x_ref.shape[0]

    def wait_buf(b):
        _wait_row_copies(hbuf.at[b], xs_hbm, 2 * tm, sem.at[b])

    @pl.when(i >= 2)
    def _():
        wait_buf(cur)

    m = mod_ref[...]
    hbuf[cur] = _rms(x_ref[...], g_ref[...]) * (1.0 + m[4:5]) + m[3:4]

    def body(r, carry):
        for u in range(DMA_UNROLL):
            rr = r * DMA_UNROLL + u
            for k in range(2):
                _row_copy(hbuf.at[cur], rr, xs_hbm, slot_ref[k * n_tok + i * tm + rr], sem.at[cur]).start()
        return carry
    lax.fori_loop(0, tm // DMA_UNROLL, body, 0)

    @pl.when(i == 0)
    def _():
        zrow[...] = jnp.zeros(zrow.shape, f32)
        n_pad = pad_ref.shape[0]

        def zstart(p, carry):
            for u in range(DMA_UNROLL):
                _row_copy(zrow, 0, xs_hbm, pad_ref[p * DMA_UNROLL + u], zsem).start()
            return carry
        lax.fori_loop(0, n_pad // DMA_UNROLL, zstart, 0)
        _wait_row_copies(zrow, xs_hbm, n_pad, zsem)

    @pl.when(i == n - 1)
    def _():
        wait_buf(cur)

        @pl.when(n >= 2)
        def _():
            wait_buf(1 - cur)


def _moe_dispatch(x1, plan, mod, g, tpb, n_batch):
    N, D = x1.shape
    grid_spec = pltpu.PrefetchScalarGridSpec(
        num_scalar_prefetch=2,
        grid=(N // ROW_TILE,),
        in_specs=[pl.BlockSpec((ROW_TILE, D), lambda t, *_: (t, 0)),
                  pl.BlockSpec((None, SUBLANES, D), lambda t, *_: (_mod_row(t, tpb, n_batch), 0, 0)),
                  pl.BlockSpec((1, D), lambda t, *_: (0, 0))],
        out_specs=pl.BlockSpec(memory_space=pl.ANY),
        scratch_shapes=[pltpu.VMEM((2, ROW_TILE, D), f32), pltpu.VMEM((SUBLANES, D), f32),
                        pltpu.SemaphoreType.DMA((2,)), pltpu.SemaphoreType.DMA(())])
    return pl.pallas_call(
        functools.partial(_moe_dispatch_kernel, n_tok=N),
        grid_spec=grid_spec,
        out_shape=jax.ShapeDtypeStruct((plan["n_slots"], D), f32),
        compiler_params=_cparams("arbitrary"),
        name="moe_dispatch",
    )(plan["slot_kmajor"], plan["pad_slots"], x1, mod, g.reshape(1, D))


def _moe_experts_kernel(te_ref, nact_ref, xs_ref, wg_ref, wu_ref, wd_ref, y_ref, wg_sc, wu_sc, wd_sc):
    i = pl.program_id(0)
    n_act = nact_ref[0]
    new_expert = jnp.logical_or(i == 0, te_ref[i] != te_ref[jnp.maximum(i - 1, 0)])

    @pl.when(jnp.logical_and(new_expert, i < n_act))
    def _():
        wg_sc[...] = wg_ref[...].astype(bf16)
        wu_sc[...] = wu_ref[...].astype(bf16)
        wd_sc[...] = wd_ref[...].astype(bf16)

    @pl.when(i < n_act)
    def _():
        x = xs_ref[...].astype(bf16)
        a = jnp.dot(x, wg_sc[...], preferred_element_type=f32)
        u = jnp.dot(x, wu_sc[...], preferred_element_type=f32)
        mid = (a * jax.nn.sigmoid(a)) * u
        y_ref[...] = jnp.dot(mid.astype(bf16), wd_sc[...], preferred_element_type=f32)

    @pl.when(i >= n_act)
    def _():
        y_ref[...] = jnp.zeros(y_ref.shape, f32)


def _moe_experts(xs, plan, wg, wu, wd):
    n_slots, D = xs.shape
    E, _, F = wg.shape

    def w_spec(shape):
        return pl.BlockSpec((None,) + shape, lambda i, te, na: (te[jnp.minimum(i, na[0] - 1)], 0, 0))

    grid_spec = pltpu.PrefetchScalarGridSpec(
        num_scalar_prefetch=2,
        grid=(n_slots // MOE_TILE,),
        in_specs=[pl.BlockSpec((MOE_TILE, D), lambda i, te, na: (jnp.minimum(i, na[0] - 1), 0)),
                  w_spec((D, F)), w_spec((D, F)), w_spec((F, D))],
        out_specs=pl.BlockSpec((MOE_TILE, D), lambda i, te, na: (i, 0)),
        scratch_shapes=[pltpu.VMEM((D, F), bf16), pltpu.VMEM((D, F), bf16), pltpu.VMEM((F, D), bf16)])
    return pl.pallas_call(
        _moe_experts_kernel,
        grid_spec=grid_spec,
        out_shape=jax.ShapeDtypeStruct((n_slots, D), f32),
        compiler_params=_cparams("arbitrary"),
        name="moe_experts",
    )(plan["tile_expert"], plan["n_active"], xs, wg, wu, wd)


def _moe_combine_kernel(slot_ref, y_hbm, x_ref, route_ref, mod_ref, ng_ref, nmod_ref, *rest, n_tok, tpb, lat_only,
                        final):
    if final:
        (o_ref, ybuf, sem) = rest
    else:
        (o_ref, h_ref, ybuf, sem) = rest
    i = pl.program_id(0)
    n = pl.num_programs(0)
    cur = i % 2
    tm = x_ref.shape[0]

    def tile_row0(j):
        if lat_only:
            return ((j // (tpb - 1)) * tpb + 1 + j % (tpb - 1)) * tm
        return j * tm

    def start(j, buf):
        def body(r, carry):
            for u in range(DMA_UNROLL):
                rr = r * DMA_UNROLL + u
                for k in range(2):
                    _row_copy(y_hbm, slot_ref[k * n_tok + tile_row0(j) + rr], ybuf.at[buf, k], rr, sem.at[buf]).start()
            return carry
        lax.fori_loop(0, tm // DMA_UNROLL, body, 0)

    @pl.when(i == 0)
    def _():
        start(0, 0)

    @pl.when(i + 1 < n)
    def _():
        start(i + 1, 1 - cur)

    _wait_row_copies(y_hbm, ybuf.at[cur, 0], 2 * tm, sem.at[cur])
    route = route_ref[...]
    moe = route[:, 2:3] * ybuf[cur, 0] + route[:, 3:4] * ybuf[cur, 1]
    xo = x_ref[...] + mod_ref[5:6, :] * moe
    if final:
        o_ref[...] = _rms(xo, ng_ref[...])
    else:
        o_ref[...] = xo
        nm = nmod_ref[...]
        h_ref[...] = (_rms(xo, ng_ref[...]) * (1.0 + nm[1:2]) + nm[0:1]).astype(h_ref.dtype)


def _moe_combine(y, plan, x1, route, mod, next_g, next_mod, tpb, n_batch, final):
    N, D = x1.shape
    lat_only = final
    lt = tpb - 1

    def tok_tile(t):
        return (t // lt) * tpb + 1 + t % lt if lat_only else t

    n_steps = n_batch * lt if lat_only else N // ROW_TILE
    row = lambda w: pl.BlockSpec((ROW_TILE, w), lambda t, s: (tok_tile(t), 0))
    modspec = pl.BlockSpec((None, SUBLANES, D), lambda t, s: (_mod_row(tok_tile(t), tpb, n_batch), 0, 0))
    out_row = pl.BlockSpec((ROW_TILE, D), lambda t, s: (t, 0))
    if final:
        out_specs, out_shape = out_row, jax.ShapeDtypeStruct((n_steps * ROW_TILE, D), f32)
    else:
        out_specs = [out_row, out_row]
        out_shape = [jax.ShapeDtypeStruct((N, D), f32), jax.ShapeDtypeStruct((N, D), bf16)]
    grid_spec = pltpu.PrefetchScalarGridSpec(
        num_scalar_prefetch=1,
        grid=(n_steps,),
        in_specs=[pl.BlockSpec(memory_space=pl.ANY), row(D), row(LANES), modspec,
                  pl.BlockSpec((1, D), lambda t, s: (0, 0)), modspec],
        out_specs=out_specs,
        scratch_shapes=[pltpu.VMEM((2, 2, ROW_TILE, D), f32), pltpu.SemaphoreType.DMA((2,))])
    return pl.pallas_call(
        functools.partial(_moe_combine_kernel, n_tok=N, tpb=tpb, lat_only=lat_only, final=final),
        grid_spec=grid_spec,
        out_shape=out_shape,
        compiler_params=_cparams("arbitrary"),
        name="moe_combine_final" if final else "moe_combine",
    )(plan["slot_kmajor"], y, x1, route, mod, next_g.reshape(1, D), next_mod)


def _rope_tables(n_ctx, n_lat):
    n_rows = n_lat // GRID_W
    rows, cols = jnp.meshgrid(jnp.arange(n_rows), jnp.arange(GRID_W), indexing='ij')
    rows = rows.reshape(-1).astype(f32)
    cols = cols.reshape(-1).astype(f32)
    axis_dim = HEAD_DIM // 2
    inv_freq = ROPE_THETA ** (-jnp.arange(0, axis_dim, 2, dtype=f32) / axis_dim)
    ar, ac = rows[:, None] * inv_freq, cols[:, None] * inv_freq
    cos = jnp.concatenate([jnp.cos(ar), jnp.cos(ar), jnp.cos(ac), jnp.cos(ac)], axis=-1)
    sin = jnp.concatenate([-jnp.sin(ar), jnp.sin(ar), -jnp.sin(ac), jnp.sin(ac)], axis=-1)
    cos = jnp.concatenate([jnp.ones((n_ctx, HEAD_DIM), f32), cos], axis=0)
    sin = jnp.concatenate([jnp.zeros((n_ctx, HEAD_DIM), f32), sin], axis=0)
    return cos, sin


def _column_plan(d_model):
    names = (("g", 3 * d_model), ("aq", A_WIDTH), ("ak", A_KV_WIDTH), ("av", A_KV_WIDTH),
             ("mq", M_WIDTH), ("mk", M_WIDTH), ("mv", M_WIDTH), ("mo", M_WIDTH),
             ("cq", C_WIDTH), ("ck", C_KV_WIDTH), ("cv", C_KV_WIDTH), ("mg", LANES))
    cols, off = {}, 0
    for name, w in names:
        cols[name] = (off, w)
        off += w
    return cols, off


def _reorder_w_in(w, d_model):
    widths = (A_WIDTH, A_KV_WIDTH, A_KV_WIDTH, M_WIDTH, M_WIDTH, M_WIDTH, M_WIDTH, N_GATE_COLS,
              C_WIDTH, C_KV_WIDTH, C_KV_WIDTH, 3 * d_model)
    names = ("aq", "ak", "av", "mq", "mk", "mv", "mo", "mg", "cq", "ck", "cv", "g")
    pieces, off = {}, 0
    for name, wd in zip(names, widths):
        pieces[name] = w[:, off:off + wd]
        off += wd
    assert off == w.shape[1]
    pieces["mg"] = jnp.pad(pieces["mg"], ((0, 0), (0, LANES - N_GATE_COLS)))
    order = ("g", "aq", "ak", "av", "mq", "mk", "mv", "mo", "cq", "ck", "cv", "mg")
    return jnp.concatenate([pieces[n] for n in order], axis=1).astype(bf16)


def kernel(x, c, ctx, c_ctx, norm1_g, norm2_g, w_mod, b_mod, w_in, a_qn_g, a_kn_g, m_conv, m_ig_b, m_fg_b,
           m_norm_g, c_sink, w_br_a, w_br_m, w_br_c, w_out, w_rg, b_rg, w_re, b_re, w_gate, w_up, w_down,
           final_g):
    B, T, D = x.shape
    Tc = ctx.shape[1]
    L = w_mod.shape[0]
    S = Tc + T
    assert Tc == ROW_TILE and T % ROW_TILE == 0 and B < SUBLANES
    tpb = S // ROW_TILE
    cols, p_width = _column_plan(D)

    xa = jnp.concatenate([ctx, x], axis=1).reshape(B * S, D)
    cond = jnp.zeros((SUBLANES, D), f32).at[:B].set(c).at[B].set(c_ctx)
    mod_all = _modulation(cond, w_mod, b_mod)
    mod_all = mod_all.reshape(L, SUBLANES, 6, D)
    mod_all = jnp.pad(mod_all, ((0, 0), (0, 0), (0, SUBLANES - 6), (0, 0)))
    cos_t, sin_t = _rope_tables(Tc, T)

    h1 = _norm_modulate(xa, mod_all[0], norm1_g[0], shift_row=0, scale_row=1, tpb=tpb, n_batch=B)
    for l in range(L):
        final = l == L - 1
        mod = mod_all[l]
        P = _matmul(h1, _reorder_w_in(w_in[l], D))
        assert P.shape[1] == p_width

        qa, ka, va, qc, kc, vc = _attn_prep(P, cols, cos_t, sin_t, a_qn_g[l], a_kn_g[l], tpb)
        oa = _attn_a(qa, ka, va, B, tpb, True, Tc, T)
        oc = _attn_c(qc, kc, vc, c_sink[l], B, tpb, Tc, T)

        gate_bias = jnp.zeros((1, LANES), f32)
        gate_bias = gate_bias.at[0, :2 * M_HEADS].set(m_ig_b[l].reshape(-1))
        gate_bias = gate_bias.at[0, 2 * M_HEADS:4 * M_HEADS].set(m_fg_b[l].reshape(-1))
        qm, km, vm, gates = _mlstm_prep(P, cols, m_conv[l], gate_bias, tpb)
        gates_t = gates.reshape(B, S, LANES)[:, :, :8 * M_HEADS].transpose(0, 2, 1)
        hf, hr = _mlstm_scan(qm, km, vm, gates, gates_t, B, Tc, T)

        x1 = _merge(oa, oc, hf, hr, P, cols, xa, mod, m_norm_g[l],
                    w_br_a[l].astype(bf16), w_br_m[l].astype(bf16), w_br_c[l].astype(bf16),
                    w_out[l].astype(bf16), tpb, B)

        w_router = jnp.pad(jnp.concatenate([w_re[l], w_rg[l]], axis=1),
                           ((0, 0), (0, LANES - N_EXPERTS - N_GROUPS)))
        b_router = jnp.pad(jnp.concatenate([b_re[l], b_rg[l]]), (0, LANES - N_EXPERTS - N_GROUPS)).reshape(1, LANES)
        route = _router(x1, mod, norm2_g[l], w_router, b_router, tpb, B)
        plan = _dispatch_plan(route)
        xs = _moe_dispatch(x1, plan, mod, norm2_g[l], tpb, B)
        y = _moe_experts(xs, plan, w_gate[l], w_up[l], w_down[l])
        if final:
            out = _moe_combine(y, plan, x1, route, mod, final_g, mod, tpb, B, True)
            return out.reshape(B, T, D)
        xa, h1 = _moe_combine(y, plan, x1, route, mod, norm1_g[l + 1], mod_all[l + 1], tpb, B, False)
```

```python
import functools

import jax
import jax.numpy as jnp
from jax import lax
from jax.experimental import pallas as pl
from jax.experimental.pallas import tpu as pltpu

f32 = jnp.float32
bf16 = jnp.bfloat16

HEAD_DIM = 128
GRID_W = 64
ROPE_THETA = 10000.0
EPS = 1e-6
NEG_INF = -1e30
A_HEADS, A_KV_HEADS = 8, 2
M_HEADS, M_CHUNK = 4, 128
C_HEADS, C_KV_HEADS = 4, 2
WINDOW = 128
N_GROUPS, EXPERTS_PER_GROUP = 4, 4
N_EXPERTS = N_GROUPS * EXPERTS_PER_GROUP

A_WIDTH = A_HEADS * HEAD_DIM
A_KV_WIDTH = A_KV_HEADS * HEAD_DIM
M_WIDTH = M_HEADS * HEAD_DIM
C_WIDTH = C_HEADS * HEAD_DIM
C_KV_WIDTH = C_KV_HEADS * HEAD_DIM
N_GATE_COLS = 4 * M_HEADS

LANES = 128
SUBLANES = 8
V7X_VMEM_BYTES = 64 * 1024 * 1024
VMEM_LIMIT = 56 * 1024 * 1024

ROW_TILE = 256
ATTN_SCALE = HEAD_DIM ** -0.5


def _cparams(*sem):
    return pltpu.CompilerParams(dimension_semantics=sem, vmem_limit_bytes=VMEM_LIMIT)


def _pick_tile(n, cap, align):
    best = align
    t = align
    while t <= min(n, cap):
        if n % t == 0:
            best = t
        t += align
    assert n % best == 0
    return best


def _mod_kernel(s_ref, w_ref, b_ref, o_ref):
    s = s_ref[...]
    s = s * jax.nn.sigmoid(s)
    o_ref[...] = jnp.dot(s.astype(bf16), w_ref[...].astype(bf16), preferred_element_type=f32) + b_ref[...]


def _modulation(cond, w_mod, b_mod):
    L, D, N6 = w_mod.shape
    tn = _pick_tile(N6, 1024, LANES)
    return pl.pallas_call(
        _mod_kernel,
        grid=(L, N6 // tn),
        in_specs=[pl.BlockSpec((SUBLANES, D), lambda l, j: (0, 0)),
                  pl.BlockSpec((None, D, tn), lambda l, j: (l, 0, j)),
                  pl.BlockSpec((None, 1, tn), lambda l, j: (l, 0, j))],
        out_specs=pl.BlockSpec((None, SUBLANES, tn), lambda l, j: (l, 0, j)),
        out_shape=jax.ShapeDtypeStruct((L, SUBLANES, N6), f32),
        compiler_params=_cparams("parallel", "parallel"),
        name="modulation",
    )(cond, w_mod, b_mod.reshape(L, 1, N6))


def _rms(x, g):
    return x * lax.rsqrt(jnp.mean(x * x, axis=-1, keepdims=True) + EPS) * g


def _normmod_kernel(x_ref, mod_ref, g_ref, o_ref, *, shift_row, scale_row):
    m = mod_ref[...]
    y = _rms(x_ref[...], g_ref[...])
    o_ref[...] = (y * (1.0 + m[scale_row:scale_row + 1]) + m[shift_row:shift_row + 1]).astype(o_ref.dtype)


def _mod_row(t, tiles_per_batch, n_batch):
    return jnp.where(t % tiles_per_batch == 0, n_batch, t // tiles_per_batch)


def _norm_modulate(xa, mod, g, *, shift_row, scale_row, tpb, n_batch):
    N, D = xa.shape
    return pl.pallas_call(
        functools.partial(_normmod_kernel, shift_row=shift_row, scale_row=scale_row),
        grid=(N // ROW_TILE,),
        in_specs=[pl.BlockSpec((ROW_TILE, D), lambda t: (t, 0)),
                  pl.BlockSpec((None, SUBLANES, D), lambda t: (_mod_row(t, tpb, n_batch), 0, 0)),
                  pl.BlockSpec((1, D), lambda t: (0, 0))],
        out_specs=pl.BlockSpec((ROW_TILE, D), lambda t: (t, 0)),
        out_shape=jax.ShapeDtypeStruct((N, D), bf16),
        compiler_params=_cparams("parallel"),
        name="norm_modulate",
    )(xa, mod, g.reshape(1, D))


def _mm_kernel(a_ref, w_ref, o_ref):
    o_ref[...] = jnp.dot(a_ref[...], w_ref[...], preferred_element_type=f32)


def _matmul(a, w):
    M, K = a.shape
    _, N = w.shape
    tm = _pick_tile(M, 1024, ROW_TILE)
    tn = _pick_tile(N, 2304, LANES)
    return pl.pallas_call(
        _mm_kernel,
        grid=(N // tn, M // tm),
        in_specs=[pl.BlockSpec((tm, K), lambda j, i: (i, 0)),
                  pl.BlockSpec((K, tn), lambda j, i: (0, j))],
        out_specs=pl.BlockSpec((tm, tn), lambda j, i: (i, j)),
        out_shape=jax.ShapeDtypeStruct((M, N), f32),
        compiler_params=_cparams("parallel", "parallel"),
        name="in_proj",
    )(a, w)


def _rope(x, cos, sin):
    lane = lax.broadcasted_iota(jnp.int32, x.shape, 1)
    swapped = jnp.where((lane % 64) < 32, pltpu.roll(x, 96, 1), pltpu.roll(x, 32, 1))
    return x * cos + swapped * sin


def _attn_prep_kernel(aq_ref, ak_ref, av_ref, cq_ref, ck_ref, cv_ref, cos_ref, sin_ref, qn_ref, kn_ref,
                      qa_o, ka_o, va_o, qc_o, kc_o, vc_o):
    cos = cos_ref[...]
    sin = sin_ref[...]
    qn = qn_ref[...]
    kn = kn_ref[...]
    for h in range(A_HEADS):
        sl = slice(h * HEAD_DIM, (h + 1) * HEAD_DIM)
        qa_o[:, sl] = (_rope(_rms(aq_ref[:, sl], qn), cos, sin) * ATTN_SCALE).astype(bf16)
    for h in range(A_KV_HEADS):
        sl = slice(h * HEAD_DIM, (h + 1) * HEAD_DIM)
        ka_o[:, sl] = _rope(_rms(ak_ref[:, sl], kn), cos, sin).astype(bf16)
    va_o[...] = av_ref[...].astype(bf16)
    for h in range(C_HEADS):
        sl = slice(h * HEAD_DIM, (h + 1) * HEAD_DIM)
        qc_o[:, sl] = _rope(cq_ref[:, sl], cos, sin).astype(bf16)
    for h in range(C_KV_HEADS):
        sl = slice(h * HEAD_DIM, (h + 1) * HEAD_DIM)
        kc_o[:, sl] = _rope(ck_ref[:, sl], cos, sin).astype(bf16)
    vc_o[...] = cv_ref[...].astype(bf16)


def _attn_prep(P, cols, cos_t, sin_t, qn, kn, tpb):
    N = P.shape[0]

    def pspec(name):
        off, w = cols[name]
        assert off % w == 0
        return pl.BlockSpec((ROW_TILE, w), lambda t, _i=off // w: (t, _i))

    def ospec(w):
        return pl.BlockSpec((ROW_TILE, w), lambda t: (t, 0))

    tab = pl.BlockSpec((ROW_TILE, HEAD_DIM), lambda t: (t % tpb, 0))
    vec = pl.BlockSpec((1, HEAD_DIM), lambda t: (0, 0))
    widths = (A_WIDTH, A_KV_WIDTH, A_KV_WIDTH, C_WIDTH, C_KV_WIDTH, C_KV_WIDTH)
    return pl.pallas_call(
        _attn_prep_kernel,
        grid=(N // ROW_TILE,),
        in_specs=[pspec("aq"), pspec("ak"), pspec("av"), pspec("cq"), pspec("ck"), pspec("cv"),
                  tab, tab, vec, vec],
        out_specs=[ospec(w) for w in widths],
        out_shape=[jax.ShapeDtypeStruct((N, w), bf16) for w in widths],
        compiler_params=_cparams("parallel"),
        name="attn_prep",
    )(P, P, P, P, P, P, cos_t, sin_t, qn.reshape(1, HEAD_DIM), kn.reshape(1, HEAD_DIM))


def _attn_a_kernel(q_ref, k_ref, v_ref, o_ref, q_sc, m_sc, l_sc, acc_sc, *, q_off, group, n_ctx, n_lat, tk):
    qi = pl.program_id(2) + q_off
    tq = q_ref.shape[0]
    for g in range(group):
        q_sc[g * tq:(g + 1) * tq, :] = q_ref[:, g * HEAD_DIM:(g + 1) * HEAD_DIM]
    m_sc[...] = jnp.full(m_sc.shape, NEG_INF, f32)
    l_sc[...] = jnp.zeros(l_sc.shape, f32)
    acc_sc[...] = jnp.zeros(acc_sc.shape, f32)

    def chunk(start, size):
        k = k_ref[pl.ds(start, size), :]
        v = v_ref[pl.ds(start, size), :]
        s = lax.dot_general(q_sc[...], k, (((1,), (1,)), ((), ())), preferred_element_type=f32)
        m_prev = m_sc[...]
        m_new = jnp.maximum(m_prev, jnp.max(s, axis=-1, keepdims=True))
        alpha = jnp.exp(m_prev - m_new)
        p = jnp.exp(s - jnp.tile(m_new, (1, size // LANES)))
        psum = p[:, 0:LANES]
        for c in range(1, size // LANES):
            psum = psum + p[:, c * LANES:(c + 1) * LANES]
        l_sc[...] = alpha * l_sc[...] + psum
        acc_sc[...] = alpha * acc_sc[...] + jnp.dot(p.astype(bf16), v, preferred_element_type=f32)
        m_sc[...] = m_new

    chunk(0, n_ctx)

    @pl.when(qi > 0)
    def _():
        def body(j, carry):
            chunk(pl.multiple_of(n_ctx + j * tk, LANES), tk)
            return carry
        lax.fori_loop(0, n_lat // tk, body, 0)

    o = acc_sc[...] / jnp.sum(l_sc[...], axis=-1, keepdims=True)
    for g in range(group):
        o_ref[:, g * HEAD_DIM:(g + 1) * HEAD_DIM] = o[g * tq:(g + 1) * tq].astype(o_ref.dtype)


def _attn_a(qa, ka, va, n_batch, tpb, need_ctx, n_ctx, n_lat):
    N = qa.shape[0]
    S = n_ctx + n_lat
    group = A_HEADS // A_KV_HEADS
    gw = group * HEAD_DIM
    rows = group * ROW_TILE
    q_off = 0 if need_ctx else 1
    tk = _pick_tile(n_lat, 1024, LANES)
    assert n_ctx % LANES == 0
    kv_spec = pl.BlockSpec((S, HEAD_DIM), lambda b, h, i: (b, h))
    q_spec = pl.BlockSpec((ROW_TILE, gw), lambda b, h, i: (b * tpb + i + q_off, h))
    return pl.pallas_call(
        functools.partial(_attn_a_kernel, q_off=q_off, group=group, n_ctx=n_ctx, n_lat=n_lat, tk=tk),
        grid=(n_batch, A_KV_HEADS, tpb - q_off),
        in_specs=[q_spec, kv_spec, kv_spec],
        out_specs=q_spec,
        out_shape=jax.ShapeDtypeStruct((N, A_WIDTH), bf16),
        scratch_shapes=[pltpu.VMEM((rows, HEAD_DIM), bf16),
                        pltpu.VMEM((rows, LANES), f32),
                        pltpu.VMEM((rows, LANES), f32),
                        pltpu.VMEM((rows, HEAD_DIM), f32)],
        compiler_params=_cparams("parallel", "parallel", "parallel"),
        name="attn_global",
    )(qa, ka, va)


def _attn_c_kernel(sink_ref, q_ref, k_ref, v_ref, o_ref, *, group, n_ctx, n_lat, span):
    kh = pl.program_id(1)
    r = pl.program_id(2)
    tq = q_ref.shape[0]
    q = q_ref[...]
    q2 = jnp.concatenate([q[:, g * HEAD_DIM:(g + 1) * HEAD_DIM] for g in range(group)], axis=0)
    lat_q0 = (r - 1) * tq
    ws = jnp.clip(lat_q0 - WINDOW, 0, n_lat - span)
    start = pl.multiple_of(n_ctx + ws, WINDOW)
    kw = k_ref[pl.ds(start, span), :]
    vw = v_ref[pl.ds(start, span), :]
    kc = k_ref[0:n_ctx, :]
    vc = v_ref[0:n_ctx, :]
    dn = (((1,), (1,)), ((), ()))
    s_loc = lax.dot_general(q2, kw, dn, preferred_element_type=f32) * ATTN_SCALE
    s_ctx = lax.dot_general(q2, kc, dn, preferred_element_type=f32) * ATTN_SCALE
    row = lax.broadcasted_iota(jnp.int32, s_loc.shape, 0)
    col = lax.broadcasted_iota(jnp.int32, s_loc.shape, 1)
    qpos = lat_q0 + row % tq
    kpos = ws + col
    valid = jnp.logical_and(jnp.abs(qpos - kpos) <= WINDOW, r > 0)
    s_loc = jnp.where(valid, s_loc, NEG_INF)
    rowc = lax.broadcasted_iota(jnp.int32, (group * tq, 1), 0)
    sink = jnp.zeros((group * tq, 1), f32)
    for g in range(group):
        sink = jnp.where(rowc // tq == g, sink_ref[kh * group + g], sink)
    m = jnp.maximum(jnp.maximum(jnp.max(s_loc, axis=-1, keepdims=True),
                                jnp.max(s_ctx, axis=-1, keepdims=True)), sink)
    p_loc = jnp.exp(s_loc - m)
    p_ctx = jnp.exp(s_ctx - m)
    den = (jnp.sum(p_loc, axis=-1, keepdims=True) + jnp.sum(p_ctx, axis=-1, keepdims=True)
           + jnp.exp(sink - m))
    o = (jnp.dot(p_loc.astype(bf16), vw, preferred_element_type=f32)
         + jnp.dot(p_ctx.astype(bf16), vc, preferred_element_type=f32)) / den
    for g in range(group):
        o_ref[:, g * HEAD_DIM:(g + 1) * HEAD_DIM] = o[g * tq:(g + 1) * tq].astype(o_ref.dtype)


def _attn_c(qc, kc, vc, sink, n_batch, tpb, n_ctx, n_lat):
    N = qc.shape[0]
    S = n_ctx + n_lat
    group = C_HEADS // C_KV_HEADS
    gw = group * HEAD_DIM
    span = ROW_TILE + 2 * WINDOW
    assert n_lat >= span and n_ctx == ROW_TILE
    kv_spec = pl.BlockSpec((S, HEAD_DIM), lambda b, h, r: (b, h))
    return pl.pallas_call(
        functools.partial(_attn_c_kernel, group=group, n_ctx=n_ctx, n_lat=n_lat, span=span),
        grid=(n_batch, C_KV_HEADS, tpb),
        in_specs=[pl.BlockSpec(memory_space=pltpu.SMEM),
                  pl.BlockSpec((ROW_TILE, gw), lambda b, h, r: (b * tpb + r, h)),
                  kv_spec, kv_spec],
        out_specs=pl.BlockSpec((ROW_TILE, gw), lambda b, h, r: (b * tpb + r, h)),
        out_shape=jax.ShapeDtypeStruct((N, C_WIDTH), bf16),
        compiler_params=_cparams("parallel", "parallel", "parallel"),
        name="attn_window",
    )(sink, qc, kc, vc)


def _chunk_cumsum(x, reverse):
    n = x.shape[0]
    pos = lax.broadcasted_iota(jnp.int32, x.shape, 0) % M_CHUNK
    k = 1
    while k < M_CHUNK:
        if reverse:
            x = x + jnp.where(pos < M_CHUNK - k, pltpu.roll(x, n - k, 0), 0.0)
        else:
            x = x + jnp.where(pos >= k, pltpu.roll(x, k, 0), 0.0)
        k *= 2
    return x


def _mlstm_prep_kernel(q_ref, qp_ref, qn_ref, k_ref, kp_ref, kn_ref, v_ref, g_ref, w_ref, gb_ref,
                       qo, ko, vo, go, *, tpb):
    r = pl.program_id(0) % tpb
    has_prev = r > 1
    has_next = jnp.logical_and(r >= 1, r < tpb - 1)
    n = q_ref.shape[0]
    row = lax.broadcasted_iota(jnp.int32, (n, 1), 0)
    w = w_ref[...]

    def conv_silu(x_ref, p_ref, n_ref, c0):
        x = x_ref[...]
        hp = jnp.where(has_prev, p_ref[SUBLANES - 1:SUBLANES, :], 0.0)
        hn = jnp.where(has_next, n_ref[0:1, :], 0.0)
        xp = jnp.where(row == 0, hp, pltpu.roll(x, 1, 0))
        xn = jnp.where(row == n - 1, hn, pltpu.roll(x, n - 1, 0))
        wc = w[:, c0:c0 + M_WIDTH]
        y = xp * wc[0:1] + x * wc[1:2] + xn * wc[2:3]
        return y * jax.nn.sigmoid(y)

    qo[...] = conv_silu(q_ref, qp_ref, qn_ref, 0).astype(bf16)
    ko[...] = (conv_silu(k_ref, kp_ref, kn_ref, M_WIDTH) * ATTN_SCALE).astype(bf16)
    vo[...] = v_ref[...].astype(bf16)

    g = g_ref[...] + gb_ref[...]
    lane = lax.broadcasted_iota(jnp.int32, g.shape, 1)
    nh2 = 2 * M_HEADS
    lf = jnp.minimum(g, 0.0) - jnp.log1p(jnp.exp(-jnp.abs(g)))
    lf = jnp.where(jnp.logical_and(lane >= nh2, lane < 2 * nh2), lf, 0.0)
    b_fwd = pltpu.roll(_chunk_cumsum(lf, False), nh2, 1)
    b_rev = pltpu.roll(_chunk_cumsum(lf, True), 2 * nh2, 1)
    go[...] = jnp.where(lane < nh2, g, jnp.where(lane < 2 * nh2, lf, jnp.where(lane < 3 * nh2, b_fwd, b_rev)))


def _mlstm_prep(P, cols, m_conv, gate_bias, tpb):
    N = P.shape[0]
    hpt = ROW_TILE // SUBLANES
    nhb = N // SUBLANES

    def main(name):
        off, w = cols[name]
        assert off % w == 0
        return pl.BlockSpec((ROW_TILE, w), lambda t, _i=off // w: (t, _i))

    def prev(name):
        off, w = cols[name]
        return pl.BlockSpec((SUBLANES, w), lambda t, _i=off // w: (jnp.maximum(t * hpt - 1, 0), _i))

    def nxt(name):
        off, w = cols[name]
        return pl.BlockSpec((SUBLANES, w), lambda t, _i=off // w: (jnp.minimum((t + 1) * hpt, nhb - 1), _i))

    ospec = pl.BlockSpec((ROW_TILE, M_WIDTH), lambda t: (t, 0))
    return pl.pallas_call(
        functools.partial(_mlstm_prep_kernel, tpb=tpb),
        grid=(N // ROW_TILE,),
        in_specs=[main("mq"), prev("mq"), nxt("mq"), main("mk"), prev("mk"), nxt("mk"), main("mv"), main("mg"),
                  pl.BlockSpec(m_conv.shape, lambda t: (0, 0)),
                  pl.BlockSpec((1, LANES), lambda t: (0, 0))],
        out_specs=[ospec, ospec, ospec, pl.BlockSpec((ROW_TILE, LANES), lambda t: (t, 0))],
        out_shape=[jax.ShapeDtypeStruct((N, M_WIDTH), bf16)] * 3 + [jax.ShapeDtypeStruct((N, LANES), f32)],
        compiler_params=_cparams("parallel"),
        name="mlstm_prep",
    )(P, P, P, P, P, P, P, P, m_conv, gate_bias)


def _mlstm_chunk_step(refs, o_ref, c_sc, n_sc, m_sc, rev):
    q_ref, k_ref, v_ref, gc_ref, gr_ref = refs
    L = M_CHUNK
    ri = lax.broadcasted_iota(jnp.int32, (L, L), 0)
    ci = lax.broadcasted_iota(jnp.int32, (L, L), 1)
    absorbed = (ci >= ri) if rev else (ci <= ri)
    gc = gc_ref[...]
    gr = gr_ref[...]
    d = 1 if rev else 0
    nt = (((1,), (1,)), ((), ()))
    tn = (((0,), (0,)), ((), ()))
    for h in range(M_HEADS):
        st = d * M_HEADS + h
        il = d * M_HEADS + h
        bl = (6 if rev else 4) * M_HEADS + d * M_HEADS + h
        i_col, b_col = gc[:, il:il + 1], gc[:, bl:bl + 1]
        i_row, b_row = gr[il:il + 1, :], gr[bl:bl + 1, :]
        sl = slice(h * HEAD_DIM, (h + 1) * HEAD_DIM)
        q, k, v = q_ref[:, sl], k_ref[:, sl], v_ref[:, sl]
        C = c_sc[st]
        nrm = n_sc[st][0:1, :]
        m = m_sc[st][0:1, 0:1]
        b_last = b_col[0:1, :] if rev else b_col[L - 1:L, :]

        log_d = jnp.where(absorbed, b_col - b_row + i_row, -jnp.inf)
        log_inter = b_col + m
        m_row = jnp.maximum(log_inter, jnp.max(log_d, axis=-1, keepdims=True))
        w_intra = jnp.exp(log_d - m_row) * lax.dot_general(q, k, nt, preferred_element_type=f32)
        w_inter = jnp.exp(log_inter - m_row)
        num = (w_inter * jnp.dot(q, C.astype(bf16), preferred_element_type=f32)
               + jnp.dot(w_intra.astype(bf16), v, preferred_element_type=f32))
        den = (w_inter * jnp.sum(q.astype(f32) * nrm, axis=-1, keepdims=True)
               + jnp.sum(w_intra, axis=-1, keepdims=True))
        o_ref[:, sl] = num / jnp.maximum(jnp.abs(den), jnp.exp(-m_row))

        log_w = b_last - b_col + i_col
        m_new = jnp.maximum(b_last + m, jnp.max(log_w, axis=0, keepdims=True))
        kw = jnp.exp(log_w - m_new) * k.astype(f32)
        decay = jnp.exp(b_last + m - m_new)
        c_sc[st] = decay * C + lax.dot_general(kw.astype(bf16), v, tn, preferred_element_type=f32)
        n_sc[st] = jnp.broadcast_to(decay * nrm + jnp.sum(kw, axis=0, keepdims=True), n_sc.shape[1:])
        m_sc[st] = jnp.broadcast_to(m_new, m_sc.shape[1:])


def _mlstm_scan_kernel(*refs):
    fwd_in, rev_in, (of_ref, or_ref, c_sc, n_sc, m_sc) = refs[0:5], refs[5:10], refs[10:]

    @pl.when(pl.program_id(1) == 0)
    def _():
        c_sc[...] = jnp.zeros(c_sc.shape, f32)
        n_sc[...] = jnp.zeros(n_sc.shape, f32)
        m_sc[...] = jnp.full(m_sc.shape, NEG_INF, f32)

    _mlstm_chunk_step(fwd_in, of_ref, c_sc, n_sc, m_sc, False)
    _mlstm_chunk_step(rev_in, or_ref, c_sc, n_sc, m_sc, True)


def _mlstm_scan(qm, km, vm, gates, gates_t, n_batch, n_ctx, n_lat):
    N = qm.shape[0]
    ncc, nlc = n_ctx // M_CHUNK, n_lat // M_CHUNK
    cpb = ncc + nlc

    def rev_chunk(s):
        return jnp.where(s < ncc, ncc - 1 - s, 2 * ncc + nlc - 1 - s)

    def specs(chunk):
        row = pl.BlockSpec((M_CHUNK, M_WIDTH), lambda b, s: (b * cpb + chunk(s), 0))
        return row, [row, row, row,
                     pl.BlockSpec((M_CHUNK, LANES), lambda b, s: (b * cpb + chunk(s), 0)),
                     pl.BlockSpec((None, gates_t.shape[1], M_CHUNK), lambda b, s: (b, 0, chunk(s)))]

    of_spec, fwd_specs = specs(lambda s: s)
    or_spec, rev_specs = specs(rev_chunk)
    args = (qm, km, vm, gates, gates_t)
    return pl.pallas_call(
        _mlstm_scan_kernel,
        grid=(n_batch, cpb),
        in_specs=fwd_specs + rev_specs,
        out_specs=[of_spec, or_spec],
        out_shape=[jax.ShapeDtypeStruct((N, M_WIDTH), f32)] * 2,
        scratch_shapes=[pltpu.VMEM((2 * M_HEADS, HEAD_DIM, HEAD_DIM), f32),
                        pltpu.VMEM((2 * M_HEADS, SUBLANES, HEAD_DIM), f32),
                        pltpu.VMEM((2 * M_HEADS, SUBLANES, LANES), f32)],
        compiler_params=_cparams("parallel", "arbitrary"),
        name="mlstm_scan",
    )(*args, *args)


def _merge_kernel(oa_ref, oc_ref, hf_ref, hr_ref, mo_ref, g_ref, x_ref, mod_ref, ng_ref,
                  wa_ref, wm_ref, wc_ref, wo_ref, o_ref, *, d_model):
    ng = ng_ref[...]
    mo = mo_ref[...]
    hsum = hf_ref[...] + hr_ref[...]
    parts = []
    for h in range(M_HEADS):
        sl = slice(h * HEAD_DIM, (h + 1) * HEAD_DIM)
        parts.append(_rms(hsum[:, sl], ng[:, sl]))
    om = (jnp.concatenate(parts, axis=-1) * jax.nn.sigmoid(mo)).astype(bf16)
    D = d_model
    ya = jnp.dot(oa_ref[...], wa_ref[...], preferred_element_type=f32)
    y = jax.nn.sigmoid(g_ref[:, 0:D]) * ya
    ym = jnp.dot(om, wm_ref[...], preferred_element_type=f32)
    y = y + jax.nn.sigmoid(g_ref[:, D:2 * D]) * ym
    yc = jnp.dot(oc_ref[...], wc_ref[...], preferred_element_type=f32)
    y = y + jax.nn.sigmoid(g_ref[:, 2 * D:3 * D]) * yc
    out = jnp.dot(y.astype(bf16), wo_ref[...], preferred_element_type=f32)
    o_ref[...] = x_ref[...] + mod_ref[2:3, :] * out


def _merge(oa, oc, hf, hr, P, cols, xa, mod, norm_g, wa, wm, wc, wo, tpb, n_batch):
    N, D = xa.shape
    tm = ROW_TILE
    sub = ROW_TILE // tm
    mo_off, mo_w = cols["mo"]
    g_off, g_w = cols["g"]
    assert mo_off % mo_w == 0 and g_off == 0

    def rows(w):
        return pl.BlockSpec((tm, w), lambda t: (t, 0))

    def whole(a):
        return pl.BlockSpec(a.shape, lambda t: (0, 0), pipeline_mode=pl.Buffered(1))

    return pl.pallas_call(
        functools.partial(_merge_kernel, d_model=D),
        grid=(N // tm,),
        in_specs=[rows(A_WIDTH), rows(C_WIDTH), rows(M_WIDTH), rows(M_WIDTH),
                  pl.BlockSpec((tm, mo_w), lambda t: (t, mo_off // mo_w)),
                  pl.BlockSpec((tm, g_w), lambda t: (t, 0)),
                  rows(D),
                  pl.BlockSpec((None, SUBLANES, D), lambda t: (_mod_row(t // sub, tpb, n_batch), 0, 0)),
                  pl.BlockSpec((1, M_WIDTH), lambda t: (0, 0)),
                  whole(wa), whole(wm), whole(wc), whole(wo)],
        out_specs=rows(D),
        out_shape=jax.ShapeDtypeStruct((N, D), f32),
        compiler_params=_cparams("parallel"),
        name="merge",
    )(oa, oc, hf, hr, P, P, xa, mod, norm_g.reshape(1, M_WIDTH), wa, wm, wc, wo)


def _router_kernel(x_ref, mod_ref, g_ref, whi_ref, wlo_ref, br_ref, route_o):
    m = mod_ref[...]
    h = _rms(x_ref[...], g_ref[...]) * (1.0 + m[4:5]) + m[3:4]
    h_hi = h.astype(bf16)
    h_lo = (h - h_hi.astype(f32)).astype(bf16)
    logits = (jnp.dot(h_hi, whi_ref[...], preferred_element_type=f32)
              + jnp.dot(h_hi, wlo_ref[...], preferred_element_type=f32)
              + jnp.dot(h_lo, whi_ref[...], preferred_element_type=f32)) + br_ref[...]
    lane = lax.broadcasted_iota(jnp.int32, logits.shape, 1)
    big = jnp.int32(LANES)
    is_g = jnp.logical_and(lane >= N_EXPERTS, lane < N_EXPERTS + N_GROUPS)
    gl = jnp.where(is_g, logits, -jnp.inf)
    gmax = jnp.max(gl, axis=-1, keepdims=True)
    g_sel = jnp.min(jnp.where(gl == gmax, lane, big), axis=-1, keepdims=True) - N_EXPERTS
    p_g = 1.0 / jnp.sum(jnp.where(is_g, jnp.exp(gl - gmax), 0.0), axis=-1, keepdims=True)
    lo = g_sel * EXPERTS_PER_GROUP
    in_grp = jnp.logical_and(lane >= lo, lane < lo + EXPERTS_PER_GROUP)
    el = jnp.where(in_grp, logits, -jnp.inf)
    e1 = jnp.max(el, axis=-1, keepdims=True)
    i1 = jnp.min(jnp.where(el == e1, lane, big), axis=-1, keepdims=True)
    el2 = jnp.where(lane == i1, -jnp.inf, el)
    e2 = jnp.max(el2, axis=-1, keepdims=True)
    i2 = jnp.min(jnp.where(el2 == e2, lane, big), axis=-1, keepdims=True)
    r = jnp.exp(e2 - e1)
    w1 = p_g / (1.0 + r)
    w2 = p_g * r / (1.0 + r)
    route_o[...] = jnp.where(lane == 0, i1.astype(f32),
                             jnp.where(lane == 1, i2.astype(f32),
                                       jnp.where(lane == 2, w1, jnp.where(lane == 3, w2, 0.0))))


def _router(x1, mod, g, w_router, b_router, tpb, n_batch):
    N, D = x1.shape
    w_hi = w_router.astype(bf16)
    w_lo = (w_router - w_hi.astype(f32)).astype(bf16)
    return pl.pallas_call(
        _router_kernel,
        grid=(N // ROW_TILE,),
        in_specs=[pl.BlockSpec((ROW_TILE, D), lambda t: (t, 0)),
                  pl.BlockSpec((None, SUBLANES, D), lambda t: (_mod_row(t, tpb, n_batch), 0, 0)),
                  pl.BlockSpec((1, D), lambda t: (0, 0)),
                  pl.BlockSpec((D, LANES), lambda t: (0, 0)),
                  pl.BlockSpec((D, LANES), lambda t: (0, 0)),
                  pl.BlockSpec((1, LANES), lambda t: (0, 0))],
        out_specs=pl.BlockSpec((ROW_TILE, LANES), lambda t: (t, 0)),
        out_shape=jax.ShapeDtypeStruct((N, LANES), f32),
        compiler_params=_cparams("parallel"),
        name="moe_router",
    )(x1, mod, g.reshape(1, D), w_hi, w_lo, b_router)


MOE_TILE = 256
DMA_UNROLL = 8


def _dispatch_plan(route):
    N = route.shape[0]
    P = 2 * N
    e_flat = route[:, 0:2].astype(jnp.int32).reshape(P)
    lanes = jnp.arange(N_EXPERTS, dtype=jnp.int32)
    onehot = (e_flat[:, None] == lanes[None, :]).astype(jnp.int32)
    csum = jnp.cumsum(onehot, axis=0)
    counts = csum[-1]
    rank = jnp.sum((csum - onehot) * onehot, axis=1)
    padded = ((counts + MOE_TILE - 1) // MOE_TILE) * MOE_TILE
    ends = jnp.cumsum(padded)
    starts = ends - padded
    slot = jnp.sum(onehot * starts[None, :], axis=1) + rank
    n_slots = P + N_EXPERTS * MOE_TILE
    n_tiles = n_slots // MOE_TILE
    tile_start = jnp.arange(n_tiles, dtype=jnp.int32) * MOE_TILE
    tile_expert = jnp.minimum(jnp.sum((tile_start[:, None] >= ends[None, :]).astype(jnp.int32), axis=1),
                              N_EXPERTS - 1)
    n_active = (ends[-1] // MOE_TILE).reshape(1)
    n_pad_e = padded - counts
    pad_off = jnp.cumsum(n_pad_e) - n_pad_e
    n_pad = jnp.sum(n_pad_e)
    q = jnp.arange(n_slots - P, dtype=jnp.int32)
    e_of_q = jnp.minimum(jnp.sum((q[:, None] >= (pad_off + n_pad_e)[None, :]).astype(jnp.int32), axis=1),
                         N_EXPERTS - 1)
    oh_q = (e_of_q[:, None] == lanes[None, :]).astype(jnp.int32)
    in_run = jnp.sum(oh_q * (starts + counts - pad_off)[None, :], axis=1) + q
    pad_slots = jnp.where(q < n_pad, in_run, ends[-1] + q - n_pad)
    slot_kmajor = slot.reshape(N, 2).T.reshape(P)
    return dict(tile_expert=tile_expert, n_active=n_active, slot_kmajor=slot_kmajor, pad_slots=pad_slots,
                n_slots=n_slots)


def _row_copy(src, src_row, dst, dst_row, sem):
    return pltpu.make_async_copy(src.at[pl.ds(src_row, 1)], dst.at[pl.ds(dst_row, 1)], sem)


def _wait_row_copies(src, dst, n_rows, sem):
    def body(r, carry):
        for u in range(DMA_UNROLL):
            _row_copy(src, 0, dst, 0, sem).wait()
        return carry
    lax.fori_loop(0, n_rows // DMA_UNROLL, body, 0)


def _moe_dispatch_kernel(slot_ref, pad_ref, x_ref, mod_ref, g_ref, xs_hbm, hbuf, zrow, sem, zsem, *, n_tok):
    i = pl.program_id(0)
    n = pl.num_programs(0)
    cur = i % 2
    tm = x_ref.shape[0]

    def wait_buf(b):
        _wait_row_copies(hbuf.at[b], xs_hbm, 2 * tm, sem.at[b])

    @pl.when(i >= 2)
    def _():
        wait_buf(cur)

    m = mod_ref[...]
    hbuf[cur] = _rms(x_ref[...], g_ref[...]) * (1.0 + m[4:5]) + m[3:4]

    for b in range(2):
        @pl.when(cur == b)
        def _(b=b):
            for rr in range(tm):
                for k in range(2):
                    _row_copy(hbuf.at[b], rr, xs_hbm, slot_ref[k * n_tok + i * tm + rr], sem.at[b]).start()

    @pl.when(i == 0)
    def _():
        zrow[...] = jnp.zeros(zrow.shape, f32)
        n_pad = pad_ref.shape[0]

        def zstart(p, carry):
            for u in range(DMA_UNROLL):
                _row_copy(zrow, 0, xs_hbm, pad_ref[p * DMA_UNROLL + u], zsem).start()
            return carry
        lax.fori_loop(0, n_pad // DMA_UNROLL, zstart, 0)
        _wait_row_copies(zrow, xs_hbm, n_pad, zsem)

    @pl.when(i == n - 1)
    def _():
        wait_buf(cur)

        @pl.when(n >= 2)
        def _():
            wait_buf(1 - cur)


def _moe_dispatch(x1, plan, mod, g, tpb, n_batch):
    N, D = x1.shape
    grid_spec = pltpu.PrefetchScalarGridSpec(
        num_scalar_prefetch=2,
        grid=(N // ROW_TILE,),
        in_specs=[pl.BlockSpec((ROW_TILE, D), lambda t, *_: (t, 0)),
                  pl.BlockSpec((None, SUBLANES, D), lambda t, *_: (_mod_row(t, tpb, n_batch), 0, 0)),
                  pl.BlockSpec((1, D), lambda t, *_: (0, 0))],
        out_specs=pl.BlockSpec(memory_space=pl.ANY),
        scratch_shapes=[pltpu.VMEM((2, ROW_TILE, D), f32), pltpu.VMEM((SUBLANES, D), f32),
                        pltpu.SemaphoreType.DMA((2,)), pltpu.SemaphoreType.DMA(())])
    return pl.pallas_call(
        functools.partial(_moe_dispatch_kernel, n_tok=N),
        grid_spec=grid_spec,
        out_shape=jax.ShapeDtypeStruct((plan["n_slots"], D), f32),
        compiler_params=_cparams("arbitrary"),
        name="moe_dispatch",
    )(plan["slot_kmajor"], plan["pad_slots"], x1, mod, g.reshape(1, D))


def _moe_experts_kernel(te_ref, nact_ref, xs_ref, wg_ref, wu_ref, wd_ref, y_ref, wg_sc, wu_sc, wd_sc):
    i = pl.program_id(0)
    n_act = nact_ref[0]
    new_expert = jnp.logical_or(i == 0, te_ref[i] != te_ref[jnp.maximum(i - 1, 0)])

    @pl.when(jnp.logical_and(new_expert, i < n_act))
    def _():
        wg_sc[...] = wg_ref[...].astype(bf16)
        wu_sc[...] = wu_ref[...].astype(bf16)
        wd_sc[...] = wd_ref[...].astype(bf16)

    @pl.when(i < n_act)
    def _():
        x = xs_ref[...].astype(bf16)
        a = jnp.dot(x, wg_sc[...], preferred_element_type=f32)
        u = jnp.dot(x, wu_sc[...], preferred_element_type=f32)
        mid = (a * jax.nn.sigmoid(a)) * u
        y_ref[...] = jnp.dot(mid.astype(bf16), wd_sc[...], preferred_element_type=f32)

    @pl.when(i >= n_act)
    def _():
        y_ref[...] = jnp.zeros(y_ref.shape, f32)


def _moe_experts(xs, plan, wg, wu, wd, layer):
    n_slots, D = xs.shape
    _, E, _, F = wg.shape

    def w_spec(shape):
        return pl.BlockSpec((None, None) + shape,
                            lambda i, te, na: (layer, te[jnp.minimum(i, na[0] - 1)], 0, 0))

    grid_spec = pltpu.PrefetchScalarGridSpec(
        num_scalar_prefetch=2,
        grid=(n_slots // MOE_TILE,),
        in_specs=[pl.BlockSpec((MOE_TILE, D), lambda i, te, na: (jnp.minimum(i, na[0] - 1), 0)),
                  w_spec((D, F)), w_spec((D, F)), w_spec((F, D))],
        out_specs=pl.BlockSpec((MOE_TILE, D), lambda i, te, na: (i, 0)),
        scratch_shapes=[pltpu.VMEM((D, F), bf16), pltpu.VMEM((D, F), bf16), pltpu.VMEM((F, D), bf16)])
    return pl.pallas_call(
        _moe_experts_kernel,
        grid_spec=grid_spec,
        out_shape=jax.ShapeDtypeStruct((n_slots, D), f32),
        compiler_params=_cparams("arbitrary"),
        name="moe_experts",
    )(plan["tile_expert"], plan["n_active"], xs, wg, wu, wd)


def _moe_combine_kernel(slot_ref, y_hbm, x_ref, route_ref, mod_ref, ng_ref, nmod_ref, *rest, n_tok, tpb, lat_only,
                        final):
    if final:
        (o_ref, ybuf, sem) = rest
    else:
        (o_ref, h_ref, ybuf, sem) = rest
    i = pl.program_id(0)
    n = pl.num_programs(0)
    cur = i % 2
    tm = x_ref.shape[0]

    def tile_row0(j):
        if lat_only:
            return ((j // (tpb - 1)) * tpb + 1 + j % (tpb - 1)) * tm
        return j * tm

    def start_rolled(j, buf):
        def body(r, carry):
            for u in range(DMA_UNROLL):
                rr = r * DMA_UNROLL + u
                for k in range(2):
                    _row_copy(y_hbm, slot_ref[k * n_tok + tile_row0(j) + rr], ybuf.at[buf, k], rr, sem.at[buf]).start()
            return carry
        lax.fori_loop(0, tm // DMA_UNROLL, body, 0)

    def start_unrolled(j, buf):
        base = tile_row0(j)
        for rr in range(tm):
            for k in range(2):
                _row_copy(y_hbm, slot_ref[k * n_tok + base + rr], ybuf.at[buf, k], rr, sem.at[buf]).start()

    @pl.when(i == 0)
    def _():
        start_rolled(0, 0)

    for b in range(2):
        @pl.when(jnp.logical_and(i + 1 < n, cur == 1 - b))
        def _(b=b):
            start_unrolled(i + 1, b)

    _wait_row_copies(y_hbm, ybuf.at[cur, 0], 2 * tm, sem.at[cur])
    route = route_ref[...]
    moe = route[:, 2:3] * ybuf[cur, 0] + route[:, 3:4] * ybuf[cur, 1]
    xo = x_ref[...] + mod_ref[5:6, :] * moe
    if final:
        o_ref[...] = _rms(xo, ng_ref[...])
    else:
        o_ref[...] = xo
        nm = nmod_ref[...]
        h_ref[...] = (_rms(xo, ng_ref[...]) * (1.0 + nm[1:2]) + nm[0:1]).astype(h_ref.dtype)


def _moe_combine(y, plan, x1, route, mod, next_g, next_mod, tpb, n_batch, final):
    N, D = x1.shape
    lat_only = final
    lt = tpb - 1

    def tok_tile(t):
        return (t // lt) * tpb + 1 + t % lt if lat_only else t

    n_steps = n_batch * lt if lat_only else N // ROW_TILE
    row = lambda w: pl.BlockSpec((ROW_TILE, w), lambda t, s: (tok_tile(t), 0))
    modspec = pl.BlockSpec((None, SUBLANES, D), lambda t, s: (_mod_row(tok_tile(t), tpb, n_batch), 0, 0))
    out_row = pl.BlockSpec((ROW_TILE, D), lambda t, s: (t, 0))
    if final:
        out_specs, out_shape = out_row, jax.ShapeDtypeStruct((n_steps * ROW_TILE, D), f32)
    else:
        out_specs = [out_row, out_row]
        out_shape = [jax.ShapeDtypeStruct((N, D), f32), jax.ShapeDtypeStruct((N, D), bf16)]
    grid_spec = pltpu.PrefetchScalarGridSpec(
        num_scalar_prefetch=1,
        grid=(n_steps,),
        in_specs=[pl.BlockSpec(memory_space=pl.ANY), row(D), row(LANES), modspec,
                  pl.BlockSpec((1, D), lambda t, s: (0, 0)), modspec],
        out_specs=out_specs,
        scratch_shapes=[pltpu.VMEM((2, 2, ROW_TILE, D), f32), pltpu.SemaphoreType.DMA((2,))])
    return pl.pallas_call(
        functools.partial(_moe_combine_kernel, n_tok=N, tpb=tpb, lat_only=lat_only, final=final),
        grid_spec=grid_spec,
        out_shape=out_shape,
        compiler_params=_cparams("arbitrary"),
        name="moe_combine_final" if final else "moe_combine",
    )(plan["slot_kmajor"], y, x1, route, mod, next_g.reshape(1, D), next_mod)


def _rope_tables(n_ctx, n_lat):
    n_rows = n_lat // GRID_W
    rows, cols = jnp.meshgrid(jnp.arange(n_rows), jnp.arange(GRID_W), indexing='ij')
    rows = rows.reshape(-1).astype(f32)
    cols = cols.reshape(-1).astype(f32)
    axis_dim = HEAD_DIM // 2
    inv_freq = ROPE_THETA ** (-jnp.arange(0, axis_dim, 2, dtype=f32) / axis_dim)
    ar, ac = rows[:, None] * inv_freq, cols[:, None] * inv_freq
    cos = jnp.concatenate([jnp.cos(ar), jnp.cos(ar), jnp.cos(ac), jnp.cos(ac)], axis=-1)
    sin = jnp.concatenate([-jnp.sin(ar), jnp.sin(ar), -jnp.sin(ac), jnp.sin(ac)], axis=-1)
    cos = jnp.concatenate([jnp.ones((n_ctx, HEAD_DIM), f32), cos], axis=0)
    sin = jnp.concatenate([jnp.zeros((n_ctx, HEAD_DIM), f32), sin], axis=0)
    return cos, sin


def _column_plan(d_model):
    names = (("g", 3 * d_model), ("aq", A_WIDTH), ("ak", A_KV_WIDTH), ("av", A_KV_WIDTH),
             ("mq", M_WIDTH), ("mk", M_WIDTH), ("mv", M_WIDTH), ("mo", M_WIDTH),
             ("cq", C_WIDTH), ("ck", C_KV_WIDTH), ("cv", C_KV_WIDTH), ("mg", LANES))
    cols, off = {}, 0
    for name, w in names:
        cols[name] = (off, w)
        off += w
    return cols, off


def _reorder_w_in(w, d_model):
    widths = (A_WIDTH, A_KV_WIDTH, A_KV_WIDTH, M_WIDTH, M_WIDTH, M_WIDTH, M_WIDTH, N_GATE_COLS,
              C_WIDTH, C_KV_WIDTH, C_KV_WIDTH, 3 * d_model)
    names = ("aq", "ak", "av", "mq", "mk", "mv", "mo", "mg", "cq", "ck", "cv", "g")
    pieces, off = {}, 0
    for name, wd in zip(names, widths):
        pieces[name] = w[:, off:off + wd]
        off += wd
    assert off == w.shape[1]
    pieces["mg"] = jnp.pad(pieces["mg"], ((0, 0), (0, LANES - N_GATE_COLS)))
    order = ("g", "aq", "ak", "av", "mq", "mk", "mv", "mo", "cq", "ck", "cv", "mg")
    return jnp.concatenate([pieces[n] for n in order], axis=1).astype(bf16)


def kernel(x, c, ctx, c_ctx, norm1_g, norm2_g, w_mod, b_mod, w_in, a_qn_g, a_kn_g, m_conv, m_ig_b, m_fg_b,
           m_norm_g, c_sink, w_br_a, w_br_m, w_br_c, w_out, w_rg, b_rg, w_re, b_re, w_gate, w_up, w_down,
           final_g):
    B, T, D = x.shape
    Tc = ctx.shape[1]
    L = w_mod.shape[0]
    S = Tc + T
    assert Tc == ROW_TILE and T % ROW_TILE == 0 and B < SUBLANES
    tpb = S // ROW_TILE
    cols, p_width = _column_plan(D)

    xa = jnp.concatenate([ctx, x], axis=1).reshape(B * S, D)
    cond = jnp.zeros((SUBLANES, D), f32).at[:B].set(c).at[B].set(c_ctx)
    mod_all = _modulation(cond, w_mod, b_mod)
    mod_all = mod_all.reshape(L, SUBLANES, 6, D)
    mod_all = jnp.pad(mod_all, ((0, 0), (0, 0), (0, SUBLANES - 6), (0, 0)))
    cos_t, sin_t = _rope_tables(Tc, T)

    h1 = _norm_modulate(xa, mod_all[0], norm1_g[0], shift_row=0, scale_row=1, tpb=tpb, n_batch=B)
    for l in range(L):
        final = l == L - 1
        mod = mod_all[l]
        P = _matmul(h1, _reorder_w_in(w_in[l], D))
        assert P.shape[1] == p_width

        qa, ka, va, qc, kc, vc = _attn_prep(P, cols, cos_t, sin_t, a_qn_g[l], a_kn_g[l], tpb)
        oa = _attn_a(qa, ka, va, B, tpb, True, Tc, T)
        oc = _attn_c(qc, kc, vc, c_sink[l], B, tpb, Tc, T)

        gate_bias = jnp.zeros((1, LANES), f32)
        gate_bias = gate_bias.at[0, :2 * M_HEADS].set(m_ig_b[l].reshape(-1))
        gate_bias = gate_bias.at[0, 2 * M_HEADS:4 * M_HEADS].set(m_fg_b[l].reshape(-1))
        qm, km, vm, gates = _mlstm_prep(P, cols, m_conv[l], gate_bias, tpb)
        gates_t = gates.reshape(B, S, LANES)[:, :, :8 * M_HEADS].transpose(0, 2, 1)
        hf, hr = _mlstm_scan(qm, km, vm, gates, gates_t, B, Tc, T)

        x1 = _merge(oa, oc, hf, hr, P, cols, xa, mod, m_norm_g[l],
                    w_br_a[l].astype(bf16), w_br_m[l].astype(bf16), w_br_c[l].astype(bf16),
                    w_out[l].astype(bf16), tpb, B)

        w_router = jnp.pad(jnp.concatenate([w_re[l], w_rg[l]], axis=1),
                           ((0, 0), (0, LANES - N_EXPERTS - N_GROUPS)))
        b_router = jnp.pad(jnp.concatenate([b_re[l], b_rg[l]]), (0, LANES - N_EXPERTS - N_GROUPS)).reshape(1, LANES)
        route = _router(x1, mod, norm2_g[l], w_router, b_router, tpb, B)
        plan = _dispatch_plan(route)
        xs = _moe_dispatch(x1, plan, mod, norm2_g[l], tpb, B)
        y = _moe_experts(xs, plan, w_gate, w_up, w_down, l)
        if final:
            out = _moe_combine(y, plan, x1, route, mod, final_g, mod, tpb, B, True)
            return out.reshape(B, T, D)
        xa, h1 = _moe_combine(y, plan, x1, route, mod, norm1_g[l + 1], mod_all[l + 1], tpb, B, False)
```

```python
import functools

import jax
import jax.numpy as jnp
from jax import lax
from jax.experimental import pallas as pl
from jax.experimental.pallas import tpu as pltpu

f32 = jnp.float32
bf16 = jnp.bfloat16

HEAD_DIM = 128
GRID_W = 64
ROPE_THETA = 10000.0
EPS = 1e-6
NEG_INF = -1e30
A_HEADS, A_KV_HEADS = 8, 2
M_HEADS, M_CHUNK = 4, 128
C_HEADS, C_KV_HEADS = 4, 2
WINDOW = 128
N_GROUPS, EXPERTS_PER_GROUP = 4, 4
N_EXPERTS = N_GROUPS * EXPERTS_PER_GROUP

A_WIDTH = A_HEADS * HEAD_DIM
A_KV_WIDTH = A_KV_HEADS * HEAD_DIM
M_WIDTH = M_HEADS * HEAD_DIM
C_WIDTH = C_HEADS * HEAD_DIM
C_KV_WIDTH = C_KV_HEADS * HEAD_DIM
N_GATE_COLS = 4 * M_HEADS

LANES = 128
SUBLANES = 8
V7X_VMEM_BYTES = 64 * 1024 * 1024
VMEM_LIMIT = 56 * 1024 * 1024

ROW_TILE = 256
ATTN_SCALE = HEAD_DIM ** -0.5


def _cparams(*sem):
    return pltpu.CompilerParams(dimension_semantics=sem, vmem_limit_bytes=VMEM_LIMIT)


def _pick_tile(n, cap, align):
    best = align
    t = align
    while t <= min(n, cap):
        if n % t == 0:
            best = t
        t += align
    assert n % best == 0
    return best


def _mod_kernel(s_ref, w_ref, b_ref, o_ref):
    s = s_ref[...]
    s = s * jax.nn.sigmoid(s)
    o_ref[...] = jnp.dot(s.astype(bf16), w_ref[...].astype(bf16), preferred_element_type=f32) + b_ref[...]


def _modulation(cond, w_mod, b_mod):
    L, D, N6 = w_mod.shape
    tn = _pick_tile(N6, 1024, LANES)
    return pl.pallas_call(
        _mod_kernel,
        grid=(L, N6 // tn),
        in_specs=[pl.BlockSpec((SUBLANES, D), lambda l, j: (0, 0)),
                  pl.BlockSpec((None, D, tn), lambda l, j: (l, 0, j)),
                  pl.BlockSpec((None, 1, tn), lambda l, j: (l, 0, j))],
        out_specs=pl.BlockSpec((None, SUBLANES, tn), lambda l, j: (l, 0, j)),
        out_shape=jax.ShapeDtypeStruct((L, SUBLANES, N6), f32),
        compiler_params=_cparams("parallel", "parallel"),
        name="modulation",
    )(cond, w_mod, b_mod.reshape(L, 1, N6))


def _rms(x, g):
    return x * lax.rsqrt(jnp.mean(x * x, axis=-1, keepdims=True) + EPS) * g


def _normmod_kernel(x_ref, mod_ref, g_ref, o_ref, *, shift_row, scale_row):
    m = mod_ref[...]
    y = _rms(x_ref[...], g_ref[...])
    o_ref[...] = (y * (1.0 + m[scale_row:scale_row + 1]) + m[shift_row:shift_row + 1]).astype(o_ref.dtype)


def _mod_row(t, tiles_per_batch, n_batch):
    return jnp.where(t % tiles_per_batch == 0, n_batch, t // tiles_per_batch)


def _norm_modulate(xa, mod, g, *, shift_row, scale_row, tpb, n_batch):
    N, D = xa.shape
    return pl.pallas_call(
        functools.partial(_normmod_kernel, shift_row=shift_row, scale_row=scale_row),
        grid=(N // ROW_TILE,),
        in_specs=[pl.BlockSpec((ROW_TILE, D), lambda t: (t, 0)),
                  pl.BlockSpec((None, SUBLANES, D), lambda t: (_mod_row(t, tpb, n_batch), 0, 0)),
                  pl.BlockSpec((1, D), lambda t: (0, 0))],
        out_specs=pl.BlockSpec((ROW_TILE, D), lambda t: (t, 0)),
        out_shape=jax.ShapeDtypeStruct((N, D), bf16),
        compiler_params=_cparams("parallel"),
        name="norm_modulate",
    )(xa, mod, g.reshape(1, D))


def _mm_kernel(a_ref, w_ref, o_ref):
    o_ref[...] = jnp.dot(a_ref[...], w_ref[...], preferred_element_type=f32)


def _matmul(a, w):
    M, K = a.shape
    _, N = w.shape
    tm = _pick_tile(M, 1024, ROW_TILE)
    tn = _pick_tile(N, 2304, LANES)
    return pl.pallas_call(
        _mm_kernel,
        grid=(N // tn, M // tm),
        in_specs=[pl.BlockSpec((tm, K), lambda j, i: (i, 0)),
                  pl.BlockSpec((K, tn), lambda j, i: (0, j))],
        out_specs=pl.BlockSpec((tm, tn), lambda j, i: (i, j)),
        out_shape=jax.ShapeDtypeStruct((M, N), f32),
        compiler_params=_cparams("parallel", "parallel"),
        name="in_proj",
    )(a, w)


def _rope(x, cos, sin):
    lane = lax.broadcasted_iota(jnp.int32, x.shape, 1)
    swapped = jnp.where((lane % 64) < 32, pltpu.roll(x, 96, 1), pltpu.roll(x, 32, 1))
    return x * cos + swapped * sin


def _attn_prep_kernel(aq_ref, ak_ref, av_ref, cq_ref, ck_ref, cv_ref, cos_ref, sin_ref, qn_ref, kn_ref,
                      qa_o, ka_o, va_o, qc_o, kc_o, vc_o):
    cos = cos_ref[...]
    sin = sin_ref[...]
    qn = qn_ref[...]
    kn = kn_ref[...]
    for h in range(A_HEADS):
        sl = slice(h * HEAD_DIM, (h + 1) * HEAD_DIM)
        qa_o[:, sl] = (_rope(_rms(aq_ref[:, sl], qn), cos, sin) * ATTN_SCALE).astype(bf16)
    for h in range(A_KV_HEADS):
        sl = slice(h * HEAD_DIM, (h + 1) * HEAD_DIM)
        ka_o[:, sl] = _rope(_rms(ak_ref[:, sl], kn), cos, sin).astype(bf16)
    va_o[...] = av_ref[...].astype(bf16)
    for h in range(C_HEADS):
        sl = slice(h * HEAD_DIM, (h + 1) * HEAD_DIM)
        qc_o[:, sl] = _rope(cq_ref[:, sl], cos, sin).astype(bf16)
    for h in range(C_KV_HEADS):
        sl = slice(h * HEAD_DIM, (h + 1) * HEAD_DIM)
        kc_o[:, sl] = _rope(ck_ref[:, sl], cos, sin).astype(bf16)
    vc_o[...] = cv_ref[...].astype(bf16)


def _attn_prep(P, cols, cos_t, sin_t, qn, kn, tpb):
    N = P.shape[0]

    def pspec(name):
        off, w = cols[name]
        assert off % w == 0
        return pl.BlockSpec((ROW_TILE, w), lambda t, _i=off // w: (t, _i))

    def ospec(w):
        return pl.BlockSpec((ROW_TILE, w), lambda t: (t, 0))

    tab = pl.BlockSpec((ROW_TILE, HEAD_DIM), lambda t: (t % tpb, 0))
    vec = pl.BlockSpec((1, HEAD_DIM), lambda t: (0, 0))
    widths = (A_WIDTH, A_KV_WIDTH, A_KV_WIDTH, C_WIDTH, C_KV_WIDTH, C_KV_WIDTH)
    return pl.pallas_call(
        _attn_prep_kernel,
        grid=(N // ROW_TILE,),
        in_specs=[pspec("aq"), pspec("ak"), pspec("av"), pspec("cq"), pspec("ck"), pspec("cv"),
                  tab, tab, vec, vec],
        out_specs=[ospec(w) for w in widths],
        out_shape=[jax.ShapeDtypeStruct((N, w), bf16) for w in widths],
        compiler_params=_cparams("parallel"),
        name="attn_prep",
    )(P, P, P, P, P, P, cos_t, sin_t, qn.reshape(1, HEAD_DIM), kn.reshape(1, HEAD_DIM))


def _attn_a_kernel(q_ref, k_ref, v_ref, o_ref, q_sc, m_sc, l_sc, acc_sc, *, q_off, group, n_ctx, n_lat, tk):
    qi = pl.program_id(2) + q_off
    tq = q_ref.shape[0]
    for g in range(group):
        q_sc[g * tq:(g + 1) * tq, :] = q_ref[:, g * HEAD_DIM:(g + 1) * HEAD_DIM]
    m_sc[...] = jnp.full(m_sc.shape, NEG_INF, f32)
    l_sc[...] = jnp.zeros(l_sc.shape, f32)
    acc_sc[...] = jnp.zeros(acc_sc.shape, f32)

    def chunk(start, size):
        k = k_ref[pl.ds(start, size), :]
        v = v_ref[pl.ds(start, size), :]
        s = lax.dot_general(q_sc[...], k, (((1,), (1,)), ((), ())), preferred_element_type=f32)
        m_prev = m_sc[...]
        m_new = jnp.maximum(m_prev, jnp.max(s, axis=-1, keepdims=True))
        alpha = jnp.exp(m_prev - m_new)
        p = jnp.exp(s - jnp.tile(m_new, (1, size // LANES)))
        psum = p[:, 0:LANES]
        for c in range(1, size // LANES):
            psum = psum + p[:, c * LANES:(c + 1) * LANES]
        l_sc[...] = alpha * l_sc[...] + psum
        acc_sc[...] = alpha * acc_sc[...] + jnp.dot(p.astype(bf16), v, preferred_element_type=f32)
        m_sc[...] = m_new

    chunk(0, n_ctx)

    @pl.when(qi > 0)
    def _():
        def body(j, carry):
            chunk(pl.multiple_of(n_ctx + j * tk, LANES), tk)
            return carry
        lax.fori_loop(0, n_lat // tk, body, 0)

    o = acc_sc[...] / jnp.sum(l_sc[...], axis=-1, keepdims=True)
    for g in range(group):
        o_ref[:, g * HEAD_DIM:(g + 1) * HEAD_DIM] = o[g * tq:(g + 1) * tq].astype(o_ref.dtype)


def _attn_a(qa, ka, va, n_batch, tpb, need_ctx, n_ctx, n_lat):
    N = qa.shape[0]
    S = n_ctx + n_lat
    group = A_HEADS // A_KV_HEADS
    gw = group * HEAD_DIM
    rows = group * ROW_TILE
    q_off = 0 if need_ctx else 1
    tk = _pick_tile(n_lat, 1024, LANES)
    assert n_ctx % LANES == 0
    kv_spec = pl.BlockSpec((S, HEAD_DIM), lambda b, h, i: (b, h))
    q_spec = pl.BlockSpec((ROW_TILE, gw), lambda b, h, i: (b * tpb + i + q_off, h))
    return pl.pallas_call(
        functools.partial(_attn_a_kernel, q_off=q_off, group=group, n_ctx=n_ctx, n_lat=n_lat, tk=tk),
        grid=(n_batch, A_KV_HEADS, tpb - q_off),
        in_specs=[q_spec, kv_spec, kv_spec],
        out_specs=q_spec,
        out_shape=jax.ShapeDtypeStruct((N, A_WIDTH), bf16),
        scratch_shapes=[pltpu.VMEM((rows, HEAD_DIM), bf16),
                        pltpu.VMEM((rows, LANES), f32),
                        pltpu.VMEM((rows, LANES), f32),
                        pltpu.VMEM((rows, HEAD_DIM), f32)],
        compiler_params=_cparams("parallel", "parallel", "parallel"),
        name="attn_global",
    )(qa, ka, va)


def _attn_c_kernel(sink_ref, q_ref, k_ref, v_ref, o_ref, *, group, n_ctx, n_lat, span):
    kh = pl.program_id(1)
    r = pl.program_id(2)
    tq = q_ref.shape[0]
    q = q_ref[...]
    q2 = jnp.concatenate([q[:, g * HEAD_DIM:(g + 1) * HEAD_DIM] for g in range(group)], axis=0)
    lat_q0 = (r - 1) * tq
    ws = jnp.clip(lat_q0 - WINDOW, 0, n_lat - span)
    start = pl.multiple_of(n_ctx + ws, WINDOW)
    kw = k_ref[pl.ds(start, span), :]
    vw = v_ref[pl.ds(start, span), :]
    kc = k_ref[0:n_ctx, :]
    vc = v_ref[0:n_ctx, :]
    dn = (((1,), (1,)), ((), ()))
    s_loc = lax.dot_general(q2, kw, dn, preferred_element_type=f32) * ATTN_SCALE
    s_ctx = lax.dot_general(q2, kc, dn, preferred_element_type=f32) * ATTN_SCALE
    row = lax.broadcasted_iota(jnp.int32, s_loc.shape, 0)
    col = lax.broadcasted_iota(jnp.int32, s_loc.shape, 1)
    qpos = lat_q0 + row % tq
    kpos = ws + col
    valid = jnp.logical_and(jnp.abs(qpos - kpos) <= WINDOW, r > 0)
    s_loc = jnp.where(valid, s_loc, NEG_INF)
    rowc = lax.broadcasted_iota(jnp.int32, (group * tq, 1), 0)
    sink = jnp.zeros((group * tq, 1), f32)
    for g in range(group):
        sink = jnp.where(rowc // tq == g, sink_ref[kh * group + g], sink)
    m = jnp.maximum(jnp.maximum(jnp.max(s_loc, axis=-1, keepdims=True),
                                jnp.max(s_ctx, axis=-1, keepdims=True)), sink)
    p_loc = jnp.exp(s_loc - m)
    p_ctx = jnp.exp(s_ctx - m)
    den = (jnp.sum(p_loc, axis=-1, keepdims=True) + jnp.sum(p_ctx, axis=-1, keepdims=True)
           + jnp.exp(sink - m))
    o = (jnp.dot(p_loc.astype(bf16), vw, preferred_element_type=f32)
         + jnp.dot(p_ctx.astype(bf16), vc, preferred_element_type=f32)) / den
    for g in range(group):
        o_ref[:, g * HEAD_DIM:(g + 1) * HEAD_DIM] = o[g * tq:(g + 1) * tq].astype(o_ref.dtype)


def _attn_c(qc, kc, vc, sink, n_batch, tpb, n_ctx, n_lat):
    N = qc.shape[0]
    S = n_ctx + n_lat
    group = C_HEADS // C_KV_HEADS
    gw = group * HEAD_DIM
    span = ROW_TILE + 2 * WINDOW
    assert n_lat >= span and n_ctx == ROW_TILE
    kv_spec = pl.BlockSpec((S, HEAD_DIM), lambda b, h, r: (b, h))
    return pl.pallas_call(
        functools.partial(_attn_c_kernel, group=group, n_ctx=n_ctx, n_lat=n_lat, span=span),
        grid=(n_batch, C_KV_HEADS, tpb),
        in_specs=[pl.BlockSpec(memory_space=pltpu.SMEM),
                  pl.BlockSpec((ROW_TILE, gw), lambda b, h, r: (b * tpb + r, h)),
                  kv_spec, kv_spec],
        out_specs=pl.BlockSpec((ROW_TILE, gw), lambda b, h, r: (b * tpb + r, h)),
        out_shape=jax.ShapeDtypeStruct((N, C_WIDTH), bf16),
        compiler_params=_cparams("parallel", "parallel", "parallel"),
        name="attn_window",
    )(sink, qc, kc, vc)


def _chunk_scan(x, reverse, op, ident):
    n = x.shape[0]
    pos = lax.broadcasted_iota(jnp.int32, x.shape, 0) % M_CHUNK
    k = 1
    while k < M_CHUNK:
        if reverse:
            x = op(x, jnp.where(pos < M_CHUNK - k, pltpu.roll(x, n - k, 0), ident))
        else:
            x = op(x, jnp.where(pos >= k, pltpu.roll(x, k, 0), ident))
        k *= 2
    return x


def _mlstm_prep_kernel(q_ref, qp_ref, qn_ref, k_ref, kp_ref, kn_ref, v_ref, g_ref, w_ref, gb_ref,
                       qo, kto, vo, go, *, tpb):
    r = pl.program_id(0) % tpb
    has_prev = r > 1
    has_next = jnp.logical_and(r >= 1, r < tpb - 1)
    n = q_ref.shape[0]
    row = lax.broadcasted_iota(jnp.int32, (n, 1), 0)
    w = w_ref[...]

    def conv_silu(x_ref, p_ref, n_ref, c0):
        x = x_ref[...]
        hp = jnp.where(has_prev, p_ref[SUBLANES - 1:SUBLANES, :], 0.0)
        hn = jnp.where(has_next, n_ref[0:1, :], 0.0)
        xp = jnp.where(row == 0, hp, pltpu.roll(x, 1, 0))
        xn = jnp.where(row == n - 1, hn, pltpu.roll(x, n - 1, 0))
        wc = w[:, c0:c0 + M_WIDTH]
        y = xp * wc[0:1] + x * wc[1:2] + xn * wc[2:3]
        return y * jax.nn.sigmoid(y)

    qo[...] = conv_silu(q_ref, qp_ref, qn_ref, 0).astype(bf16)
    vo[...] = v_ref[...].astype(bf16)
    kf = conv_silu(k_ref, kp_ref, kn_ref, M_WIDTH) * ATTN_SCALE
    for c in range(n // M_CHUNK):
        for h in range(M_HEADS):
            blk = kf[c * M_CHUNK:(c + 1) * M_CHUNK, h * HEAD_DIM:(h + 1) * HEAD_DIM]
            r0 = (c * M_HEADS + h) * HEAD_DIM
            kto[r0:r0 + HEAD_DIM, :] = blk.T.astype(bf16)

    g = g_ref[...] + gb_ref[...]
    lane = lax.broadcasted_iota(jnp.int32, g.shape, 1)
    nh2 = 2 * M_HEADS
    lf = pltpu.roll(jnp.minimum(g, 0.0) - jnp.log1p(jnp.exp(-jnp.abs(g))), LANES - nh2, 1)
    fwd = lane < M_HEADS
    b = jnp.where(fwd, _chunk_scan(lf, False, jnp.add, 0.0), _chunk_scan(lf, True, jnp.add, 0.0))
    r = g - b
    cm = jnp.where(fwd, _chunk_scan(r, False, jnp.maximum, -jnp.inf), _chunk_scan(r, True, jnp.maximum, -jnp.inf))
    go[...] = jnp.where(lane < nh2, b,
                        jnp.where(lane < 2 * nh2, pltpu.roll(r, nh2, 1),
                                  jnp.where(lane < 3 * nh2, pltpu.roll(cm, 2 * nh2, 1), 0.0)))


def _mlstm_prep(P, cols, m_conv, gate_bias, tpb):
    N = P.shape[0]
    hpt = ROW_TILE // SUBLANES
    nhb = N // SUBLANES

    def main(name):
        off, w = cols[name]
        assert off % w == 0
        return pl.BlockSpec((ROW_TILE, w), lambda t, _i=off // w: (t, _i))

    def prev(name):
        off, w = cols[name]
        return pl.BlockSpec((SUBLANES, w), lambda t, _i=off // w: (jnp.maximum(t * hpt - 1, 0), _i))

    def nxt(name):
        off, w = cols[name]
        return pl.BlockSpec((SUBLANES, w), lambda t, _i=off // w: (jnp.minimum((t + 1) * hpt, nhb - 1), _i))

    ospec = pl.BlockSpec((ROW_TILE, M_WIDTH), lambda t: (t, 0))
    kt_rows = (ROW_TILE // M_CHUNK) * M_WIDTH
    return pl.pallas_call(
        functools.partial(_mlstm_prep_kernel, tpb=tpb),
        grid=(N // ROW_TILE,),
        in_specs=[main("mq"), prev("mq"), nxt("mq"), main("mk"), prev("mk"), nxt("mk"), main("mv"), main("mg"),
                  pl.BlockSpec(m_conv.shape, lambda t: (0, 0)),
                  pl.BlockSpec((1, LANES), lambda t: (0, 0))],
        out_specs=[ospec, pl.BlockSpec((kt_rows, M_CHUNK), lambda t: (t, 0)), ospec,
                   pl.BlockSpec((ROW_TILE, LANES), lambda t: (t, 0))],
        out_shape=[jax.ShapeDtypeStruct((N, M_WIDTH), bf16),
                   jax.ShapeDtypeStruct((N // M_CHUNK * M_WIDTH, M_CHUNK), bf16),
                   jax.ShapeDtypeStruct((N, M_WIDTH), bf16),
                   jax.ShapeDtypeStruct((N, LANES), f32)],
        compiler_params=_cparams("parallel"),
        name="mlstm_prep",
    )(P, P, P, P, P, P, P, P, m_conv, gate_bias)


def _mlstm_chunk_step(refs, o_ref, c_sc, m_sc, rev):
    q_ref, kt_ref, v_ref, gc_ref, gr_ref = refs
    L = M_CHUNK
    ri = lax.broadcasted_iota(jnp.int32, (L, L), 0)
    ci = lax.broadcasted_iota(jnp.int32, (L, L), 1)
    absorbed = (ci >= ri) if rev else (ci <= ri)
    lane = lax.broadcasted_iota(jnp.int32, (L, HEAD_DIM), 1)
    ones_col = jnp.where(lane == 0, 1.0, 0.0).astype(bf16)
    gc = gc_ref[...]
    gr = gr_ref[...]
    d = 1 if rev else 0
    nh2 = 2 * M_HEADS
    last = 0 if rev else L - 1
    for h in range(M_HEADS):
        st = d * M_HEADS + h
        b_col = gc[:, st:st + 1]
        cm_col = gc[:, 2 * nh2 + st:2 * nh2 + st + 1]
        r_row = gr[nh2 + st:nh2 + st + 1, :]
        sl = slice(h * HEAD_DIM, (h + 1) * HEAD_DIM)
        q = q_ref[:, sl]
        kt = kt_ref[h * HEAD_DIM:(h + 1) * HEAD_DIM, :]
        v_aug = jnp.concatenate([v_ref[:, sl], ones_col], axis=1)
        c_aug = c_sc[st]
        m = m_sc[st][0:1, 0:1]
        r_max = cm_col[last:last + 1, :]

        mm_col = jnp.maximum(m, cm_col)
        w_intra = (jnp.exp(jnp.where(absorbed, r_row - mm_col, -jnp.inf))
                   * jnp.dot(q, kt, preferred_element_type=f32))
        w_inter = jnp.exp(m - mm_col)
        tot = (w_inter * jnp.dot(q, c_aug.astype(bf16), preferred_element_type=f32)
               + jnp.dot(w_intra.astype(bf16), v_aug, preferred_element_type=f32))
        den = tot[:, HEAD_DIM:HEAD_DIM + 1]
        o_ref[:, sl] = tot[:, 0:HEAD_DIM] / jnp.maximum(jnp.abs(den), jnp.exp(-(b_col + mm_col)))

        m_top = jnp.maximum(m, r_max)
        kw_t = (kt.astype(f32) * jnp.exp(r_row - m_top)).astype(bf16)
        c_sc[st] = jnp.exp(m - m_top) * c_aug + jnp.dot(kw_t, v_aug, preferred_element_type=f32)
        m_sc[st] = jnp.broadcast_to(b_col[last:last + 1, :] + m_top, m_sc.shape[1:])


def _mlstm_scan_kernel(*refs):
    fwd_in, rev_in, (of_ref, or_ref, c_sc, m_sc) = refs[0:5], refs[5:10], refs[10:]

    @pl.when(pl.program_id(1) == 0)
    def _():
        c_sc[...] = jnp.zeros(c_sc.shape, f32)
        m_sc[...] = jnp.full(m_sc.shape, NEG_INF, f32)

    _mlstm_chunk_step(fwd_in, of_ref, c_sc, m_sc, False)
    _mlstm_chunk_step(rev_in, or_ref, c_sc, m_sc, True)


def _mlstm_scan(qm, kt, vm, gates, gates_t, n_batch, n_ctx, n_lat):
    N = qm.shape[0]
    ncc, nlc = n_ctx // M_CHUNK, n_lat // M_CHUNK
    cpb = ncc + nlc

    def rev_chunk(s):
        return jnp.where(s < ncc, ncc - 1 - s, 2 * ncc + nlc - 1 - s)

    def specs(chunk):
        row = pl.BlockSpec((M_CHUNK, M_WIDTH), lambda b, s: (b * cpb + chunk(s), 0))
        return row, [row,
                     pl.BlockSpec((M_WIDTH, M_CHUNK), lambda b, s: (b * cpb + chunk(s), 0)),
                     row,
                     pl.BlockSpec((M_CHUNK, LANES), lambda b, s: (b * cpb + chunk(s), 0)),
                     pl.BlockSpec((None, gates_t.shape[1], M_CHUNK), lambda b, s: (b, 0, chunk(s)))]

    of_spec, fwd_specs = specs(lambda s: s)
    or_spec, rev_specs = specs(rev_chunk)
    args = (qm, kt, vm, gates, gates_t)
    return pl.pallas_call(
        _mlstm_scan_kernel,
        grid=(n_batch, cpb),
        in_specs=fwd_specs + rev_specs,
        out_specs=[of_spec, or_spec],
        out_shape=[jax.ShapeDtypeStruct((N, M_WIDTH), f32)] * 2,
        scratch_shapes=[pltpu.VMEM((2 * M_HEADS, HEAD_DIM, 2 * HEAD_DIM), f32),
                        pltpu.VMEM((2 * M_HEADS, SUBLANES, LANES), f32)],
        compiler_params=_cparams("parallel", "arbitrary"),
        name="mlstm_scan",
    )(*args, *args)


def _merge_kernel(oa_ref, oc_ref, hf_ref, hr_ref, mo_ref, g_ref, x_ref, mod_ref, ng_ref,
                  wa_ref, wm_ref, wc_ref, wo_ref, o_ref, *, d_model):
    ng = ng_ref[...]
    mo = mo_ref[...]
    hsum = hf_ref[...] + hr_ref[...]
    parts = []
    for h in range(M_HEADS):
        sl = slice(h * HEAD_DIM, (h + 1) * HEAD_DIM)
        parts.append(_rms(hsum[:, sl], ng[:, sl]))
    om = (jnp.concatenate(parts, axis=-1) * jax.nn.sigmoid(mo)).astype(bf16)
    D = d_model
    ya = jnp.dot(oa_ref[...], wa_ref[...], preferred_element_type=f32)
    y = jax.nn.sigmoid(g_ref[:, 0:D]) * ya
    ym = jnp.dot(om, wm_ref[...], preferred_element_type=f32)
    y = y + jax.nn.sigmoid(g_ref[:, D:2 * D]) * ym
    yc = jnp.dot(oc_ref[...], wc_ref[...], preferred_element_type=f32)
    y = y + jax.nn.sigmoid(g_ref[:, 2 * D:3 * D]) * yc
    out = jnp.dot(y.astype(bf16), wo_ref[...], preferred_element_type=f32)
    o_ref[...] = x_ref[...] + mod_ref[2:3, :] * out


def _merge(oa, oc, hf, hr, P, cols, xa, mod, norm_g, wa, wm, wc, wo, tpb, n_batch):
    N, D = xa.shape
    tm = ROW_TILE
    sub = ROW_TILE // tm
    mo_off, mo_w = cols["mo"]
    g_off, g_w = cols["g"]
    assert mo_off % mo_w == 0 and g_off == 0

    def rows(w):
        return pl.BlockSpec((tm, w), lambda t: (t, 0))

    def whole(a):
        return pl.BlockSpec(a.shape, lambda t: (0, 0), pipeline_mode=pl.Buffered(1))

    return pl.pallas_call(
        functools.partial(_merge_kernel, d_model=D),
        grid=(N // tm,),
        in_specs=[rows(A_WIDTH), rows(C_WIDTH), rows(M_WIDTH), rows(M_WIDTH),
                  pl.BlockSpec((tm, mo_w), lambda t: (t, mo_off // mo_w)),
                  pl.BlockSpec((tm, g_w), lambda t: (t, 0)),
                  rows(D),
                  pl.BlockSpec((None, SUBLANES, D), lambda t: (_mod_row(t // sub, tpb, n_batch), 0, 0)),
                  pl.BlockSpec((1, M_WIDTH), lambda t: (0, 0)),
                  whole(wa), whole(wm), whole(wc), whole(wo)],
        out_specs=rows(D),
        out_shape=jax.ShapeDtypeStruct((N, D), f32),
        compiler_params=_cparams("parallel"),
        name="merge",
    )(oa, oc, hf, hr, P, P, xa, mod, norm_g.reshape(1, M_WIDTH), wa, wm, wc, wo)


def _router_kernel(x_ref, mod_ref, g_ref, whi_ref, wlo_ref, br_ref, route_o):
    m = mod_ref[...]
    h = _rms(x_ref[...], g_ref[...]) * (1.0 + m[4:5]) + m[3:4]
    h_hi = h.astype(bf16)
    h_lo = (h - h_hi.astype(f32)).astype(bf16)
    logits = (jnp.dot(h_hi, whi_ref[...], preferred_element_type=f32)
              + jnp.dot(h_hi, wlo_ref[...], preferred_element_type=f32)
              + jnp.dot(h_lo, whi_ref[...], preferred_element_type=f32)) + br_ref[...]
    lane = lax.broadcasted_iota(jnp.int32, logits.shape, 1)
    big = jnp.int32(LANES)
    is_g = jnp.logical_and(lane >= N_EXPERTS, lane < N_EXPERTS + N_GROUPS)
    gl = jnp.where(is_g, logits, -jnp.inf)
    gmax = jnp.max(gl, axis=-1, keepdims=True)
    g_sel = jnp.min(jnp.where(gl == gmax, lane, big), axis=-1, keepdims=True) - N_EXPERTS
    p_g = 1.0 / jnp.sum(jnp.where(is_g, jnp.exp(gl - gmax), 0.0), axis=-1, keepdims=True)
    lo = g_sel * EXPERTS_PER_GROUP
    in_grp = jnp.logical_and(lane >= lo, lane < lo + EXPERTS_PER_GROUP)
    el = jnp.where(in_grp, logits, -jnp.inf)
    e1 = jnp.max(el, axis=-1, keepdims=True)
    i1 = jnp.min(jnp.where(el == e1, lane, big), axis=-1, keepdims=True)
    el2 = jnp.where(lane == i1, -jnp.inf, el)
    e2 = jnp.max(el2, axis=-1, keepdims=True)
    i2 = jnp.min(jnp.where(el2 == e2, lane, big), axis=-1, keepdims=True)
    r = jnp.exp(e2 - e1)
    w1 = p_g / (1.0 + r)
    w2 = p_g * r / (1.0 + r)
    route_o[...] = jnp.where(lane == 0, i1.astype(f32),
                             jnp.where(lane == 1, i2.astype(f32),
                                       jnp.where(lane == 2, w1, jnp.where(lane == 3, w2, 0.0))))


def _router(x1, mod, g, w_router, b_router, tpb, n_batch):
    N, D = x1.shape
    w_hi = w_router.astype(bf16)
    w_lo = (w_router - w_hi.astype(f32)).astype(bf16)
    return pl.pallas_call(
        _router_kernel,
        grid=(N // ROW_TILE,),
        in_specs=[pl.BlockSpec((ROW_TILE, D), lambda t: (t, 0)),
                  pl.BlockSpec((None, SUBLANES, D), lambda t: (_mod_row(t, tpb, n_batch), 0, 0)),
                  pl.BlockSpec((1, D), lambda t: (0, 0)),
                  pl.BlockSpec((D, LANES), lambda t: (0, 0)),
                  pl.BlockSpec((D, LANES), lambda t: (0, 0)),
                  pl.BlockSpec((1, LANES), lambda t: (0, 0))],
        out_specs=pl.BlockSpec((ROW_TILE, LANES), lambda t: (t, 0)),
        out_shape=jax.ShapeDtypeStruct((N, LANES), f32),
        compiler_params=_cparams("parallel"),
        name="moe_router",
    )(x1, mod, g.reshape(1, D), w_hi, w_lo, b_router)


MOE_TILE = 256
DMA_UNROLL = 8


def _dispatch_plan(route):
    N = route.shape[0]
    P = 2 * N
    e_flat = route[:, 0:2].astype(jnp.int32).reshape(P)
    lanes = jnp.arange(N_EXPERTS, dtype=jnp.int32)
    onehot = (e_flat[:, None] == lanes[None, :]).astype(jnp.int32)
    csum = jnp.cumsum(onehot, axis=0)
    counts = csum[-1]
    rank = jnp.sum((csum - onehot) * onehot, axis=1)
    padded = ((counts + MOE_TILE - 1) // MOE_TILE) * MOE_TILE
    ends = jnp.cumsum(padded)
    starts = ends - padded
    slot = jnp.sum(onehot * starts[None, :], axis=1) + rank
    n_slots = P + N_EXPERTS * MOE_TILE
    n_tiles = n_slots // MOE_TILE
    tile_start = jnp.arange(n_tiles, dtype=jnp.int32) * MOE_TILE
    tile_expert = jnp.minimum(jnp.sum((tile_start[:, None] >= ends[None, :]).astype(jnp.int32), axis=1),
                              N_EXPERTS - 1)
    n_active = (ends[-1] // MOE_TILE).reshape(1)
    n_pad_e = padded - counts
    pad_off = jnp.cumsum(n_pad_e) - n_pad_e
    n_pad = jnp.sum(n_pad_e)
    q = jnp.arange(n_slots - P, dtype=jnp.int32)
    e_of_q = jnp.minimum(jnp.sum((q[:, None] >= (pad_off + n_pad_e)[None, :]).astype(jnp.int32), axis=1),
                         N_EXPERTS - 1)
    oh_q = (e_of_q[:, None] == lanes[None, :]).astype(jnp.int32)
    in_run = jnp.sum(oh_q * (starts + counts - pad_off)[None, :], axis=1) + q
    pad_slots = jnp.where(q < n_pad, in_run, ends[-1] + q - n_pad)
    slot_kmajor = slot.reshape(N, 2).T.reshape(P)
    return dict(tile_expert=tile_expert, n_active=n_active, slot_kmajor=slot_kmajor, pad_slots=pad_slots,
                n_slots=n_slots)


def _row_copy(src, src_row, dst, dst_row, sem):
    return pltpu.make_async_copy(src.at[pl.ds(src_row, 1)], dst.at[pl.ds(dst_row, 1)], sem)


def _wait_row_copies(src, dst, n_rows, sem):
    def body(r, carry):
        for u in range(DMA_UNROLL):
            _row_copy(src, 0, dst, 0, sem).wait()
        return carry
    lax.fori_loop(0, n_rows // DMA_UNROLL, body, 0)


def _moe_dispatch_kernel(slot_ref, pad_ref, x_ref, mod_ref, g_ref, xs_hbm, hbuf, zrow, sem, zsem, *, n_tok):
    i = pl.program_id(0)
    n = pl.num_programs(0)
    cur = i % 2
    tm = x_ref.shape[0]

    def wait_buf(b):
        _wait_row_copies(hbuf.at[b], xs_hbm, 2 * tm, sem.at[b])

    @pl.when(i >= 2)
    def _():
        wait_buf(cur)

    m = mod_ref[...]
    hbuf[cur] = _rms(x_ref[...], g_ref[...]) * (1.0 + m[4:5]) + m[3:4]

    for b in range(2):
        @pl.when(cur == b)
        def _(b=b):
            for rr in range(tm):
                for k in range(2):
                    _row_copy(hbuf.at[b], rr, xs_hbm, slot_ref[k * n_tok + i * tm + rr], sem.at[b]).start()

    @pl.when(i == 0)
    def _():
        zrow[...] = jnp.zeros(zrow.shape, f32)
        n_pad = pad_ref.shape[0]

        def zstart(p, carry):
            for u in range(DMA_UNROLL):
                _row_copy(zrow, 0, xs_hbm, pad_ref[p * DMA_UNROLL + u], zsem).start()
            return carry
        lax.fori_loop(0, n_pad // DMA_UNROLL, zstart, 0)
        _wait_row_copies(zrow, xs_hbm, n_pad, zsem)

    @pl.when(i == n - 1)
    def _():
        wait_buf(cur)

        @pl.when(n >= 2)
        def _():
            wait_buf(1 - cur)


def _moe_dispatch(x1, plan, mod, g, tpb, n_batch):
    N, D = x1.shape
    grid_spec = pltpu.PrefetchScalarGridSpec(
        num_scalar_prefetch=2,
        grid=(N // ROW_TILE,),
        in_specs=[pl.BlockSpec((ROW_TILE, D), lambda t, *_: (t, 0)),
                  pl.BlockSpec((None, SUBLANES, D), lambda t, *_: (_mod_row(t, tpb, n_batch), 0, 0)),
                  pl.BlockSpec((1, D), lambda t, *_: (0, 0))],
        out_specs=pl.BlockSpec(memory_space=pl.ANY),
        scratch_shapes=[pltpu.VMEM((2, ROW_TILE, D), f32), pltpu.VMEM((SUBLANES, D), f32),
                        pltpu.SemaphoreType.DMA((2,)), pltpu.SemaphoreType.DMA(())])
    return pl.pallas_call(
        functools.partial(_moe_dispatch_kernel, n_tok=N),
        grid_spec=grid_spec,
        out_shape=jax.ShapeDtypeStruct((plan["n_slots"], D), f32),
        compiler_params=_cparams("arbitrary"),
        name="moe_dispatch",
    )(plan["slot_kmajor"], plan["pad_slots"], x1, mod, g.reshape(1, D))


def _moe_experts_kernel(te_ref, nact_ref, xs_ref, wg_ref, wu_ref, wd_ref, y_ref, wg_sc, wu_sc, wd_sc):
    i = pl.program_id(0)
    n_act = nact_ref[0]
    new_expert = jnp.logical_or(i == 0, te_ref[i] != te_ref[jnp.maximum(i - 1, 0)])

    @pl.when(jnp.logical_and(new_expert, i < n_act))
    def _():
        wg_sc[...] = wg_ref[...].astype(bf16)
        wu_sc[...] = wu_ref[...].astype(bf16)
        wd_sc[...] = wd_ref[...].astype(bf16)

    @pl.when(i < n_act)
    def _():
        x = xs_ref[...].astype(bf16)
        a = jnp.dot(x, wg_sc[...], preferred_element_type=f32)
        u = jnp.dot(x, wu_sc[...], preferred_element_type=f32)
        mid = (a * jax.nn.sigmoid(a)) * u
        y_ref[...] = jnp.dot(mid.astype(bf16), wd_sc[...], preferred_element_type=f32)

    @pl.when(i >= n_act)
    def _():
        y_ref[...] = jnp.zeros(y_ref.shape, f32)


def _moe_experts(xs, plan, wg, wu, wd, layer):
    n_slots, D = xs.shape
    _, E, _, F = wg.shape

    def w_spec(shape):
        return pl.BlockSpec((None, None) + shape,
                            lambda i, te, na: (layer, te[jnp.minimum(i, na[0] - 1)], 0, 0))

    grid_spec = pltpu.PrefetchScalarGridSpec(
        num_scalar_prefetch=2,
        grid=(n_slots // MOE_TILE,),
        in_specs=[pl.BlockSpec((MOE_TILE, D), lambda i, te, na: (jnp.minimum(i, na[0] - 1), 0)),
                  w_spec((D, F)), w_spec((D, F)), w_spec((F, D))],
        out_specs=pl.BlockSpec((MOE_TILE, D), lambda i, te, na: (i, 0)),
        scratch_shapes=[pltpu.VMEM((D, F), bf16), pltpu.VMEM((D, F), bf16), pltpu.VMEM((F, D), bf16)])
    return pl.pallas_call(
        _moe_experts_kernel,
        grid_spec=grid_spec,
        out_shape=jax.ShapeDtypeStruct((n_slots, D), f32),
        compiler_params=_cparams("arbitrary"),
        name="moe_experts",
    )(plan["tile_expert"], plan["n_active"], xs, wg, wu, wd)


def _moe_combine_kernel(slot_ref, y_hbm, x_ref, route_ref, mod_ref, ng_ref, nmod_ref, *rest, n_tok, tpb, lat_only,
                        final):
    if final:
        (o_ref, ybuf, sem) = rest
    else:
        (o_ref, h_ref, ybuf, sem) = rest
    i = pl.program_id(0)
    n = pl.num_programs(0)
    cur = i % 2
    tm = x_ref.shape[0]

    def tile_row0(j):
        if lat_only:
            return ((j // (tpb - 1)) * tpb + 1 + j % (tpb - 1)) * tm
        return j * tm

    def start_rolled(j, buf):
        def body(r, carry):
            for u in range(DMA_UNROLL):
                rr = r * DMA_UNROLL + u
                for k in range(2):
                    _row_copy(y_hbm, slot_ref[k * n_tok + tile_row0(j) + rr], ybuf.at[buf, k], rr, sem.at[buf]).start()
            return carry
        lax.fori_loop(0, tm // DMA_UNROLL, body, 0)

    def start_unrolled(j, buf):
        base = tile_row0(j)
        for rr in range(tm):
            for k in range(2):
                _row_copy(y_hbm, slot_ref[k * n_tok + base + rr], ybuf.at[buf, k], rr, sem.at[buf]).start()

    @pl.when(i == 0)
    def _():
        start_rolled(0, 0)

    for b in range(2):
        @pl.when(jnp.logical_and(i + 1 < n, cur == 1 - b))
        def _(b=b):
            start_unrolled(i + 1, b)

    _wait_row_copies(y_hbm, ybuf.at[cur, 0], 2 * tm, sem.at[cur])
    route = route_ref[...]
    moe = route[:, 2:3] * ybuf[cur, 0] + route[:, 3:4] * ybuf[cur, 1]
    xo = x_ref[...] + mod_ref[5:6, :] * moe
    if final:
        o_ref[...] = _rms(xo, ng_ref[...])
    else:
        o_ref[...] = xo
        nm = nmod_ref[...]
        h_ref[...] = (_rms(xo, ng_ref[...]) * (1.0 + nm[1:2]) + nm[0:1]).astype(h_ref.dtype)


def _moe_combine(y, plan, x1, route, mod, next_g, next_mod, tpb, n_batch, final):
    N, D = x1.shape
    lat_only = final
    lt = tpb - 1

    def tok_tile(t):
        return (t // lt) * tpb + 1 + t % lt if lat_only else t

    n_steps = n_batch * lt if lat_only else N // ROW_TILE
    row = lambda w: pl.BlockSpec((ROW_TILE, w), lambda t, s: (tok_tile(t), 0))
    modspec = pl.BlockSpec((None, SUBLANES, D), lambda t, s: (_mod_row(tok_tile(t), tpb, n_batch), 0, 0))
    out_row = pl.BlockSpec((ROW_TILE, D), lambda t, s: (t, 0))
    if final:
        out_specs, out_shape = out_row, jax.ShapeDtypeStruct((n_steps * ROW_TILE, D), f32)
    else:
        out_specs = [out_row, out_row]
        out_shape = [jax.ShapeDtypeStruct((N, D), f32), jax.ShapeDtypeStruct((N, D), bf16)]
    grid_spec = pltpu.PrefetchScalarGridSpec(
        num_scalar_prefetch=1,
        grid=(n_steps,),
        in_specs=[pl.BlockSpec(memory_space=pl.ANY), row(D), row(LANES), modspec,
                  pl.BlockSpec((1, D), lambda t, s: (0, 0)), modspec],
        out_specs=out_specs,
        scratch_shapes=[pltpu.VMEM((2, 2, ROW_TILE, D), f32), pltpu.SemaphoreType.DMA((2,))])
    return pl.pallas_call(
        functools.partial(_moe_combine_kernel, n_tok=N, tpb=tpb, lat_only=lat_only, final=final),
        grid_spec=grid_spec,
        out_shape=out_shape,
        compiler_params=_cparams("arbitrary"),
        name="moe_combine_final" if final else "moe_combine",
    )(plan["slot_kmajor"], y, x1, route, mod, next_g.reshape(1, D), next_mod)


def _rope_tables(n_ctx, n_lat):
    n_rows = n_lat // GRID_W
    rows, cols = jnp.meshgrid(jnp.arange(n_rows), jnp.arange(GRID_W), indexing='ij')
    rows = rows.reshape(-1).astype(f32)
    cols = cols.reshape(-1).astype(f32)
    axis_dim = HEAD_DIM // 2
    inv_freq = ROPE_THETA ** (-jnp.arange(0, axis_dim, 2, dtype=f32) / axis_dim)
    ar, ac = rows[:, None] * inv_freq, cols[:, None] * inv_freq
    cos = jnp.concatenate([jnp.cos(ar), jnp.cos(ar), jnp.cos(ac), jnp.cos(ac)], axis=-1)
    sin = jnp.concatenate([-jnp.sin(ar), jnp.sin(ar), -jnp.sin(ac), jnp.sin(ac)], axis=-1)
    cos = jnp.concatenate([jnp.ones((n_ctx, HEAD_DIM), f32), cos], axis=0)
    sin = jnp.concatenate([jnp.zeros((n_ctx, HEAD_DIM), f32), sin], axis=0)
    return cos, sin


def _column_plan(d_model):
    names = (("g", 3 * d_model), ("aq", A_WIDTH), ("ak", A_KV_WIDTH), ("av", A_KV_WIDTH),
             ("mq", M_WIDTH), ("mk", M_WIDTH), ("mv", M_WIDTH), ("mo", M_WIDTH),
             ("cq", C_WIDTH), ("ck", C_KV_WIDTH), ("cv", C_KV_WIDTH), ("mg", LANES))
    cols, off = {}, 0
    for name, w in names:
        cols[name] = (off, w)
        off += w
    return cols, off


def _reorder_w_in(w, d_model):
    widths = (A_WIDTH, A_KV_WIDTH, A_KV_WIDTH, M_WIDTH, M_WIDTH, M_WIDTH, M_WIDTH, N_GATE_COLS,
              C_WIDTH, C_KV_WIDTH, C_KV_WIDTH, 3 * d_model)
    names = ("aq", "ak", "av", "mq", "mk", "mv", "mo", "mg", "cq", "ck", "cv", "g")
    pieces, off = {}, 0
    for name, wd in zip(names, widths):
        pieces[name] = w[:, off:off + wd]
        off += wd
    assert off == w.shape[1]
    pieces["mg"] = jnp.pad(pieces["mg"], ((0, 0), (0, LANES - N_GATE_COLS)))
    order = ("g", "aq", "ak", "av", "mq", "mk", "mv", "mo", "cq", "ck", "cv", "mg")
    return jnp.concatenate([pieces[n] for n in order], axis=1).astype(bf16)


def kernel(x, c, ctx, c_ctx, norm1_g, norm2_g, w_mod, b_mod, w_in, a_qn_g, a_kn_g, m_conv, m_ig_b, m_fg_b,
           m_norm_g, c_sink, w_br_a, w_br_m, w_br_c, w_out, w_rg, b_rg, w_re, b_re, w_gate, w_up, w_down,
           final_g):
    B, T, D = x.shape
    Tc = ctx.shape[1]
    L = w_mod.shape[0]
    S = Tc + T
    assert Tc == ROW_TILE and T % ROW_TILE == 0 and B < SUBLANES
    tpb = S // ROW_TILE
    cols, p_width = _column_plan(D)

    xa = jnp.concatenate([ctx, x], axis=1).reshape(B * S, D)
    cond = jnp.zeros((SUBLANES, D), f32).at[:B].set(c).at[B].set(c_ctx)
    mod_all = _modulation(cond, w_mod, b_mod)
    mod_all = mod_all.reshape(L, SUBLANES, 6, D)
    mod_all = jnp.pad(mod_all, ((0, 0), (0, 0), (0, SUBLANES - 6), (0, 0)))
    cos_t, sin_t = _rope_tables(Tc, T)

    h1 = _norm_modulate(xa, mod_all[0], norm1_g[0], shift_row=0, scale_row=1, tpb=tpb, n_batch=B)
    for l in range(L):
        final = l == L - 1
        mod = mod_all[l]
        P = _matmul(h1, _reorder_w_in(w_in[l], D))
        assert P.shape[1] == p_width

        qa, ka, va, qc, kc, vc = _attn_prep(P, cols, cos_t, sin_t, a_qn_g[l], a_kn_g[l], tpb)
        oa = _attn_a(qa, ka, va, B, tpb, True, Tc, T)
        oc = _attn_c(qc, kc, vc, c_sink[l], B, tpb, Tc, T)

        gate_bias = jnp.zeros((1, LANES), f32)
        gate_bias = gate_bias.at[0, :2 * M_HEADS].set(m_ig_b[l].reshape(-1))
        gate_bias = gate_bias.at[0, 2 * M_HEADS:4 * M_HEADS].set(m_fg_b[l].reshape(-1))
        qm, km, vm, gates = _mlstm_prep(P, cols, m_conv[l], gate_bias, tpb)
        gates_t = gates.reshape(B, S, LANES)[:, :, :8 * M_HEADS].transpose(0, 2, 1)
        hf, hr = _mlstm_scan(qm, km, vm, gates, gates_t, B, Tc, T)

        x1 = _merge(oa, oc, hf, hr, P, cols, xa, mod, m_norm_g[l],
                    w_br_a[l].astype(bf16), w_br_m[l].astype(bf16), w_br_c[l].astype(bf16),
                    w_out[l].astype(bf16), tpb, B)

        w_router = jnp.pad(jnp.concatenate([w_re[l], w_rg[l]], axis=1),
                           ((0, 0), (0, LANES - N_EXPERTS - N_GROUPS)))
        b_router = jnp.pad(jnp.concatenate([b_re[l], b_rg[l]]), (0, LANES - N_EXPERTS - N_GROUPS)).reshape(1, LANES)
        route = _router(x1, mod, norm2_g[l], w_router, b_router, tpb, B)
        plan = _dispatch_plan(route)
        xs = _moe_dispatch(x1, plan, mod, norm2_g[l], tpb, B)
        y = _moe_experts(xs, plan, w_gate, w_up, w_down, l)
        if final:
            out = _moe_combine(y, plan, x1, route, mod, final_g, mod, tpb, B, True)
            return out.reshape(B, T, D)
        xa, h1 = _moe_combine(y, plan, x1, route, mod, norm1_g[l + 1], mod_all[l + 1], tpb, B, False)
```

```python
import functools

import jax
import jax.numpy as jnp
from jax import lax
from jax.experimental import pallas as pl
from jax.experimental.pallas import tpu as pltpu

f32 = jnp.float32
bf16 = jnp.bfloat16

HEAD_DIM = 128
GRID_W = 64
ROPE_THETA = 10000.0
EPS = 1e-6
NEG_INF = -1e30
A_HEADS, A_KV_HEADS = 8, 2
M_HEADS, M_CHUNK = 4, 128
C_HEADS, C_KV_HEADS = 4, 2
WINDOW = 128
N_GROUPS, EXPERTS_PER_GROUP = 4, 4
N_EXPERTS = N_GROUPS * EXPERTS_PER_GROUP

A_WIDTH = A_HEADS * HEAD_DIM
A_KV_WIDTH = A_KV_HEADS * HEAD_DIM
M_WIDTH = M_HEADS * HEAD_DIM
C_WIDTH = C_HEADS * HEAD_DIM
C_KV_WIDTH = C_KV_HEADS * HEAD_DIM
N_GATE_COLS = 4 * M_HEADS

LANES = 128
SUBLANES = 8
V7X_VMEM_BYTES = 64 * 1024 * 1024
VMEM_LIMIT = 56 * 1024 * 1024

ROW_TILE = 256
ATTN_SCALE = HEAD_DIM ** -0.5


def _cparams(*sem):
    return pltpu.CompilerParams(dimension_semantics=sem, vmem_limit_bytes=VMEM_LIMIT)


def _pick_tile(n, cap, align):
    best = align
    t = align
    while t <= min(n, cap):
        if n % t == 0:
            best = t
        t += align
    assert n % best == 0
    return best


def _mod_kernel(s_ref, w_ref, b_ref, o_ref):
    s = s_ref[...]
    s = s * jax.nn.sigmoid(s)
    o_ref[...] = jnp.dot(s.astype(bf16), w_ref[...].astype(bf16), preferred_element_type=f32) + b_ref[...]


def _modulation(cond, w_mod, b_mod):
    L, D, N6 = w_mod.shape
    tn = _pick_tile(N6, 1024, LANES)
    return pl.pallas_call(
        _mod_kernel,
        grid=(L, N6 // tn),
        in_specs=[pl.BlockSpec((SUBLANES, D), lambda l, j: (0, 0)),
                  pl.BlockSpec((None, D, tn), lambda l, j: (l, 0, j)),
                  pl.BlockSpec((None, 1, tn), lambda l, j: (l, 0, j))],
        out_specs=pl.BlockSpec((None, SUBLANES, tn), lambda l, j: (l, 0, j)),
        out_shape=jax.ShapeDtypeStruct((L, SUBLANES, N6), f32),
        compiler_params=_cparams("parallel", "parallel"),
        name="modulation",
    )(cond, w_mod, b_mod.reshape(L, 1, N6))


def _rms(x, g):
    return x * lax.rsqrt(jnp.mean(x * x, axis=-1, keepdims=True) + EPS) * g


def _split_token_specs(d_model, tpb, sub=1):
    lt = tpb - 1
    rows = ROW_TILE // sub

    def lat(i):
        t = i // sub
        return ((t // tpb) * lt + jnp.maximum(t % tpb - 1, 0)) * sub + i % sub, 0

    def ctx(i):
        t = i // sub
        return (t // tpb) * sub + i % sub, 0

    return [pl.BlockSpec((rows, d_model), lat), pl.BlockSpec((rows, d_model), ctx)]


def _select_tokens(x_ref, c_ref, step, tpb, sub=1):
    return jnp.where((step // sub) % tpb == 0, c_ref[...], x_ref[...])


def _normmod_kernel(x_ref, c_ref, mod_ref, g_ref, o_ref, *, shift_row, scale_row, tpb):
    m = mod_ref[...]
    y = _rms(_select_tokens(x_ref, c_ref, pl.program_id(0), tpb), g_ref[...])
    o_ref[...] = (y * (1.0 + m[scale_row:scale_row + 1]) + m[shift_row:shift_row + 1]).astype(o_ref.dtype)


def _mod_row(t, tiles_per_batch, n_batch):
    return jnp.where(t % tiles_per_batch == 0, n_batch, t // tiles_per_batch)


def _norm_modulate(x2, ctx2, mod, g, *, shift_row, scale_row, tpb, n_batch):
    D = x2.shape[1]
    N = x2.shape[0] + ctx2.shape[0]
    return pl.pallas_call(
        functools.partial(_normmod_kernel, shift_row=shift_row, scale_row=scale_row, tpb=tpb),
        grid=(N // ROW_TILE,),
        in_specs=_split_token_specs(D, tpb) + [
            pl.BlockSpec((None, SUBLANES, D), lambda t: (_mod_row(t, tpb, n_batch), 0, 0)),
            pl.BlockSpec((1, D), lambda t: (0, 0))],
        out_specs=pl.BlockSpec((ROW_TILE, D), lambda t: (t, 0)),
        out_shape=jax.ShapeDtypeStruct((N, D), bf16),
        compiler_params=_cparams("parallel"),
        name="norm_modulate",
    )(x2, ctx2, mod, g.reshape(1, D))


def _mm_kernel(a_ref, w_ref, o_ref):
    o_ref[...] = jnp.dot(a_ref[...], w_ref[...], preferred_element_type=f32)


def _matmul(a, w):
    M, K = a.shape
    _, N = w.shape
    tm = _pick_tile(M, 1024, ROW_TILE)
    tn = _pick_tile(N, 2304, LANES)
    return pl.pallas_call(
        _mm_kernel,
        grid=(N // tn, M // tm),
        in_specs=[pl.BlockSpec((tm, K), lambda j, i: (i, 0)),
                  pl.BlockSpec((K, tn), lambda j, i: (0, j))],
        out_specs=pl.BlockSpec((tm, tn), lambda j, i: (i, j)),
        out_shape=jax.ShapeDtypeStruct((M, N), f32),
        compiler_params=_cparams("parallel", "parallel"),
        name="in_proj",
    )(a, w)


def _rope(x, cos, sin):
    lane = lax.broadcasted_iota(jnp.int32, x.shape, 1)
    swapped = jnp.where((lane % 64) < 32, pltpu.roll(x, 96, 1), pltpu.roll(x, 32, 1))
    return x * cos + swapped * sin


def _attn_prep_kernel(aq_ref, ak_ref, av_ref, cq_ref, ck_ref, cv_ref, cos_ref, sin_ref, qn_ref, kn_ref,
                      qa_o, ka_o, va_o, qc_o, kc_o, vc_o):
    cos = cos_ref[...]
    sin = sin_ref[...]
    qn = qn_ref[...]
    kn = kn_ref[...]
    for h in range(A_HEADS):
        sl = slice(h * HEAD_DIM, (h + 1) * HEAD_DIM)
        qa_o[:, sl] = (_rope(_rms(aq_ref[:, sl], qn), cos, sin) * ATTN_SCALE).astype(bf16)
    for h in range(A_KV_HEADS):
        sl = slice(h * HEAD_DIM, (h + 1) * HEAD_DIM)
        ka_o[:, sl] = _rope(_rms(ak_ref[:, sl], kn), cos, sin).astype(bf16)
    va_o[...] = av_ref[...].astype(bf16)
    for h in range(C_HEADS):
        sl = slice(h * HEAD_DIM, (h + 1) * HEAD_DIM)
        qc_o[:, sl] = _rope(cq_ref[:, sl], cos, sin).astype(bf16)
    for h in range(C_KV_HEADS):
        sl = slice(h * HEAD_DIM, (h + 1) * HEAD_DIM)
        kc_o[:, sl] = _rope(ck_ref[:, sl], cos, sin).astype(bf16)
    vc_o[...] = cv_ref[...].astype(bf16)


def _attn_prep(P, cols, cos_t, sin_t, qn, kn, tpb):
    N = P.shape[0]

    def pspec(name):
        off, w = cols[name]
        assert off % w == 0
        return pl.BlockSpec((ROW_TILE, w), lambda t, _i=off // w: (t, _i))

    def ospec(w):
        return pl.BlockSpec((ROW_TILE, w), lambda t: (t, 0))

    tab = pl.BlockSpec((ROW_TILE, HEAD_DIM), lambda t: (t % tpb, 0))
    vec = pl.BlockSpec((1, HEAD_DIM), lambda t: (0, 0))
    widths = (A_WIDTH, A_KV_WIDTH, A_KV_WIDTH, C_WIDTH, C_KV_WIDTH, C_KV_WIDTH)
    return pl.pallas_call(
        _attn_prep_kernel,
        grid=(N // ROW_TILE,),
        in_specs=[pspec("aq"), pspec("ak"), pspec("av"), pspec("cq"), pspec("ck"), pspec("cv"),
                  tab, tab, vec, vec],
        out_specs=[ospec(w) for w in widths],
        out_shape=[jax.ShapeDtypeStruct((N, w), bf16) for w in widths],
        compiler_params=_cparams("parallel"),
        name="attn_prep",
    )(P, P, P, P, P, P, cos_t, sin_t, qn.reshape(1, HEAD_DIM), kn.reshape(1, HEAD_DIM))


def _attn_a_kernel(q_ref, k_ref, v_ref, o_ref, q_sc, m_sc, l_sc, acc_sc, *, q_off, group, n_ctx, n_lat, tk):
    qi = pl.program_id(2) + q_off
    tq = q_ref.shape[0]
    for g in range(group):
        q_sc[g * tq:(g + 1) * tq, :] = q_ref[:, g * HEAD_DIM:(g + 1) * HEAD_DIM]
    m_sc[...] = jnp.full(m_sc.shape, NEG_INF, f32)
    l_sc[...] = jnp.zeros(l_sc.shape, f32)
    acc_sc[...] = jnp.zeros(acc_sc.shape, f32)

    def chunk(start, size):
        k = k_ref[pl.ds(start, size), :]
        v = v_ref[pl.ds(start, size), :]
        s = lax.dot_general(q_sc[...], k, (((1,), (1,)), ((), ())), preferred_element_type=f32)
        m_prev = m_sc[...]
        m_new = jnp.maximum(m_prev, jnp.max(s, axis=-1, keepdims=True))
        alpha = jnp.exp(m_prev - m_new)
        p = jnp.exp(s - jnp.tile(m_new, (1, size // LANES)))
        psum = p[:, 0:LANES]
        for c in range(1, size // LANES):
            psum = psum + p[:, c * LANES:(c + 1) * LANES]
        l_sc[...] = alpha * l_sc[...] + psum
        acc_sc[...] = alpha * acc_sc[...] + jnp.dot(p.astype(bf16), v, preferred_element_type=f32)
        m_sc[...] = m_new

    chunk(0, n_ctx)

    @pl.when(qi > 0)
    def _():
        def body(j, carry):
            chunk(pl.multiple_of(n_ctx + j * tk, LANES), tk)
            return carry
        lax.fori_loop(0, n_lat // tk, body, 0)

    o = acc_sc[...] / jnp.sum(l_sc[...], axis=-1, keepdims=True)
    for g in range(group):
        o_ref[:, g * HEAD_DIM:(g + 1) * HEAD_DIM] = o[g * tq:(g + 1) * tq].astype(o_ref.dtype)


def _attn_a(qa, ka, va, n_batch, tpb, need_ctx, n_ctx, n_lat):
    N = qa.shape[0]
    S = n_ctx + n_lat
    group = A_HEADS // A_KV_HEADS
    gw = group * HEAD_DIM
    rows = group * ROW_TILE
    q_off = 0 if need_ctx else 1
    tk = _pick_tile(n_lat, 1024, LANES)
    assert n_ctx % LANES == 0
    kv_spec = pl.BlockSpec((S, HEAD_DIM), lambda b, h, i: (b, h))
    q_spec = pl.BlockSpec((ROW_TILE, gw), lambda b, h, i: (b * tpb + i + q_off, h))
    return pl.pallas_call(
        functools.partial(_attn_a_kernel, q_off=q_off, group=group, n_ctx=n_ctx, n_lat=n_lat, tk=tk),
        grid=(n_batch, A_KV_HEADS, tpb - q_off),
        in_specs=[q_spec, kv_spec, kv_spec],
        out_specs=q_spec,
        out_shape=jax.ShapeDtypeStruct((N, A_WIDTH), bf16),
        scratch_shapes=[pltpu.VMEM((rows, HEAD_DIM), bf16),
                        pltpu.VMEM((rows, LANES), f32),
                        pltpu.VMEM((rows, LANES), f32),
                        pltpu.VMEM((rows, HEAD_DIM), f32)],
        compiler_params=_cparams("parallel", "parallel", "parallel"),
        name="attn_global",
    )(qa, ka, va)


def _attn_c_kernel(sink_ref, q_ref, k_ref, v_ref, o_ref, *, group, n_ctx, n_lat, span):
    kh = pl.program_id(1)
    r = pl.program_id(2)
    tq = q_ref.shape[0]
    q = q_ref[...]
    q2 = jnp.concatenate([q[:, g * HEAD_DIM:(g + 1) * HEAD_DIM] for g in range(group)], axis=0)
    lat_q0 = (r - 1) * tq
    ws = jnp.clip(lat_q0 - WINDOW, 0, n_lat - span)
    start = pl.multiple_of(n_ctx + ws, WINDOW)
    kw = k_ref[pl.ds(start, span), :]
    vw = v_ref[pl.ds(start, span), :]
    kc = k_ref[0:n_ctx, :]
    vc = v_ref[0:n_ctx, :]
    dn = (((1,), (1,)), ((), ()))
    s_loc = lax.dot_general(q2, kw, dn, preferred_element_type=f32) * ATTN_SCALE
    s_ctx = lax.dot_general(q2, kc, dn, preferred_element_type=f32) * ATTN_SCALE
    row = lax.broadcasted_iota(jnp.int32, s_loc.shape, 0)
    col = lax.broadcasted_iota(jnp.int32, s_loc.shape, 1)
    qpos = lat_q0 + row % tq
    kpos = ws + col
    valid = jnp.logical_and(jnp.abs(qpos - kpos) <= WINDOW, r > 0)
    s_loc = jnp.where(valid, s_loc, NEG_INF)
    rowc = lax.broadcasted_iota(jnp.int32, (group * tq, 1), 0)
    sink = jnp.zeros((group * tq, 1), f32)
    for g in range(group):
        sink = jnp.where(rowc // tq == g, sink_ref[kh * group + g], sink)
    m = jnp.maximum(jnp.maximum(jnp.max(s_loc, axis=-1, keepdims=True),
                                jnp.max(s_ctx, axis=-1, keepdims=True)), sink)
    p_loc = jnp.exp(s_loc - m)
    p_ctx = jnp.exp(s_ctx - m)
    den = (jnp.sum(p_loc, axis=-1, keepdims=True) + jnp.sum(p_ctx, axis=-1, keepdims=True)
           + jnp.exp(sink - m))
    o = (jnp.dot(p_loc.astype(bf16), vw, preferred_element_type=f32)
         + jnp.dot(p_ctx.astype(bf16), vc, preferred_element_type=f32)) / den
    for g in range(group):
        o_ref[:, g * HEAD_DIM:(g + 1) * HEAD_DIM] = o[g * tq:(g + 1) * tq].astype(o_ref.dtype)


def _attn_c(qc, kc, vc, sink, n_batch, tpb, n_ctx, n_lat):
    N = qc.shape[0]
    S = n_ctx + n_lat
    group = C_HEADS // C_KV_HEADS
    gw = group * HEAD_DIM
    span = ROW_TILE + 2 * WINDOW
    assert n_lat >= span and n_ctx == ROW_TILE
    kv_spec = pl.BlockSpec((S, HEAD_DIM), lambda b, h, r: (b, h))
    return pl.pallas_call(
        functools.partial(_attn_c_kernel, group=group, n_ctx=n_ctx, n_lat=n_lat, span=span),
        grid=(n_batch, C_KV_HEADS, tpb),
        in_specs=[pl.BlockSpec(memory_space=pltpu.SMEM),
                  pl.BlockSpec((ROW_TILE, gw), lambda b, h, r: (b * tpb + r, h)),
                  kv_spec, kv_spec],
        out_specs=pl.BlockSpec((ROW_TILE, gw), lambda b, h, r: (b * tpb + r, h)),
        out_shape=jax.ShapeDtypeStruct((N, C_WIDTH), bf16),
        compiler_params=_cparams("parallel", "parallel", "parallel"),
        name="attn_window",
    )(sink, qc, kc, vc)


def _chunk_scan(x, reverse, op, ident):
    n = x.shape[0]
    pos = lax.broadcasted_iota(jnp.int32, x.shape, 0) % M_CHUNK
    k = 1
    while k < M_CHUNK:
        if reverse:
            x = op(x, jnp.where(pos < M_CHUNK - k, pltpu.roll(x, n - k, 0), ident))
        else:
            x = op(x, jnp.where(pos >= k, pltpu.roll(x, k, 0), ident))
        k *= 2
    return x


def _mlstm_prep_kernel(q_ref, qp_ref, qn_ref, k_ref, kp_ref, kn_ref, v_ref, g_ref, w_ref, gb_ref,
                       qo, kto, vo, go, *, tpb):
    r = pl.program_id(0) % tpb
    has_prev = r > 1
    has_next = jnp.logical_and(r >= 1, r < tpb - 1)
    n = q_ref.shape[0]
    row = lax.broadcasted_iota(jnp.int32, (n, 1), 0)
    w = w_ref[...]

    def conv_silu(x_ref, p_ref, n_ref, c0):
        x = x_ref[...]
        hp = jnp.where(has_prev, p_ref[SUBLANES - 1:SUBLANES, :], 0.0)
        hn = jnp.where(has_next, n_ref[0:1, :], 0.0)
        xp = jnp.where(row == 0, hp, pltpu.roll(x, 1, 0))
        xn = jnp.where(row == n - 1, hn, pltpu.roll(x, n - 1, 0))
        wc = w[:, c0:c0 + M_WIDTH]
        y = xp * wc[0:1] + x * wc[1:2] + xn * wc[2:3]
        return y * jax.nn.sigmoid(y)

    qo[...] = conv_silu(q_ref, qp_ref, qn_ref, 0).astype(bf16)
    vo[...] = v_ref[...].astype(bf16)
    kf = conv_silu(k_ref, kp_ref, kn_ref, M_WIDTH) * ATTN_SCALE
    for c in range(n // M_CHUNK):
        for h in range(M_HEADS):
            blk = kf[c * M_CHUNK:(c + 1) * M_CHUNK, h * HEAD_DIM:(h + 1) * HEAD_DIM]
            r0 = (c * M_HEADS + h) * HEAD_DIM
            kto[r0:r0 + HEAD_DIM, :] = blk.T.astype(bf16)

    g = g_ref[...] + gb_ref[...]
    lane = lax.broadcasted_iota(jnp.int32, g.shape, 1)
    nh2 = 2 * M_HEADS
    lf = pltpu.roll(jnp.minimum(g, 0.0) - jnp.log1p(jnp.exp(-jnp.abs(g))), LANES - nh2, 1)
    fwd = lane < M_HEADS
    b = jnp.where(fwd, _chunk_scan(lf, False, jnp.add, 0.0), _chunk_scan(lf, True, jnp.add, 0.0))
    r = g - b
    cm = jnp.where(fwd, _chunk_scan(r, False, jnp.maximum, -jnp.inf), _chunk_scan(r, True, jnp.maximum, -jnp.inf))
    go[...] = jnp.where(lane < nh2, b,
                        jnp.where(lane < 2 * nh2, pltpu.roll(r, nh2, 1),
                                  jnp.where(lane < 3 * nh2, pltpu.roll(cm, 2 * nh2, 1), 0.0)))


def _mlstm_prep(P, cols, m_conv, gate_bias, tpb):
    N = P.shape[0]
    hpt = ROW_TILE // SUBLANES
    nhb = N // SUBLANES

    def main(name):
        off, w = cols[name]
        assert off % w == 0
        return pl.BlockSpec((ROW_TILE, w), lambda t, _i=off // w: (t, _i))

    def prev(name):
        off, w = cols[name]
        return pl.BlockSpec((SUBLANES, w), lambda t, _i=off // w: (jnp.maximum(t * hpt - 1, 0), _i))

    def nxt(name):
        off, w = cols[name]
        return pl.BlockSpec((SUBLANES, w), lambda t, _i=off // w: (jnp.minimum((t + 1) * hpt, nhb - 1), _i))

    ospec = pl.BlockSpec((ROW_TILE, M_WIDTH), lambda t: (t, 0))
    kt_rows = (ROW_TILE // M_CHUNK) * M_WIDTH
    return pl.pallas_call(
        functools.partial(_mlstm_prep_kernel, tpb=tpb),
        grid=(N // ROW_TILE,),
        in_specs=[main("mq"), prev("mq"), nxt("mq"), main("mk"), prev("mk"), nxt("mk"), main("mv"), main("mg"),
                  pl.BlockSpec(m_conv.shape, lambda t: (0, 0)),
                  pl.BlockSpec((1, LANES), lambda t: (0, 0))],
        out_specs=[ospec, pl.BlockSpec((kt_rows, M_CHUNK), lambda t: (t, 0)), ospec,
                   pl.BlockSpec((ROW_TILE, LANES), lambda t: (t, 0))],
        out_shape=[jax.ShapeDtypeStruct((N, M_WIDTH), bf16),
                   jax.ShapeDtypeStruct((N // M_CHUNK * M_WIDTH, M_CHUNK), bf16),
                   jax.ShapeDtypeStruct((N, M_WIDTH), bf16),
                   jax.ShapeDtypeStruct((N, LANES), f32)],
        compiler_params=_cparams("parallel"),
        name="mlstm_prep",
    )(P, P, P, P, P, P, P, P, m_conv, gate_bias)


def _mlstm_chunk_step(refs, o_ref, c_sc, m_sc, rev):
    q_ref, kt_ref, v_ref, gc_ref, gr_ref = refs
    L = M_CHUNK
    ri = lax.broadcasted_iota(jnp.int32, (L, L), 0)
    ci = lax.broadcasted_iota(jnp.int32, (L, L), 1)
    absorbed = (ci >= ri) if rev else (ci <= ri)
    lane = lax.broadcasted_iota(jnp.int32, (L, HEAD_DIM), 1)
    ones_col = jnp.where(lane == 0, 1.0, 0.0).astype(bf16)
    gc = gc_ref[...]
    gr = gr_ref[...]
    d = 1 if rev else 0
    nh2 = 2 * M_HEADS
    last = 0 if rev else L - 1
    for h in range(M_HEADS):
        st = d * M_HEADS + h
        b_col = gc[:, st:st + 1]
        cm_col = gc[:, 2 * nh2 + st:2 * nh2 + st + 1]
        r_row = gr[nh2 + st:nh2 + st + 1, :]
        sl = slice(h * HEAD_DIM, (h + 1) * HEAD_DIM)
        q = q_ref[:, sl]
        kt = kt_ref[h * HEAD_DIM:(h + 1) * HEAD_DIM, :]
        v_aug = jnp.concatenate([v_ref[:, sl], ones_col], axis=1)
        c_aug = c_sc[st]
        m = m_sc[st][0:1, 0:1]
        r_max = cm_col[last:last + 1, :]

        mm_col = jnp.maximum(m, cm_col)
        w_intra = (jnp.exp(jnp.where(absorbed, r_row - mm_col, -jnp.inf))
                   * jnp.dot(q, kt, preferred_element_type=f32))
        w_inter = jnp.exp(m - mm_col)
        tot = (w_inter * jnp.dot(q, c_aug.astype(bf16), preferred_element_type=f32)
               + jnp.dot(w_intra.astype(bf16), v_aug, preferred_element_type=f32))
        den = tot[:, HEAD_DIM:HEAD_DIM + 1]
        o_ref[:, sl] = tot[:, 0:HEAD_DIM] / jnp.maximum(jnp.abs(den), jnp.exp(-(b_col + mm_col)))

        m_top = jnp.maximum(m, r_max)
        kw_t = (kt.astype(f32) * jnp.exp(r_row - m_top)).astype(bf16)
        c_sc[st] = jnp.exp(m - m_top) * c_aug + jnp.dot(kw_t, v_aug, preferred_element_type=f32)
        m_sc[st] = jnp.broadcast_to(b_col[last:last + 1, :] + m_top, m_sc.shape[1:])


def _mlstm_scan_kernel(*refs):
    fwd_in, rev_in, (of_ref, or_ref, c_sc, m_sc) = refs[0:5], refs[5:10], refs[10:]

    @pl.when(pl.program_id(1) == 0)
    def _():
        c_sc[...] = jnp.zeros(c_sc.shape, f32)
        m_sc[...] = jnp.full(m_sc.shape, NEG_INF, f32)

    _mlstm_chunk_step(fwd_in, of_ref, c_sc, m_sc, False)
    _mlstm_chunk_step(rev_in, or_ref, c_sc, m_sc, True)


def _mlstm_scan(qm, kt, vm, gates, gates_t, n_batch, n_ctx, n_lat):
    N = qm.shape[0]
    ncc, nlc = n_ctx // M_CHUNK, n_lat // M_CHUNK
    cpb = ncc + nlc

    def rev_chunk(s):
        return jnp.where(s < ncc, ncc - 1 - s, 2 * ncc + nlc - 1 - s)

    def specs(chunk):
        row = pl.BlockSpec((M_CHUNK, M_WIDTH), lambda b, s: (b * cpb + chunk(s), 0))
        return row, [row,
                     pl.BlockSpec((M_WIDTH, M_CHUNK), lambda b, s: (b * cpb + chunk(s), 0)),
                     row,
                     pl.BlockSpec((M_CHUNK, LANES), lambda b, s: (b * cpb + chunk(s), 0)),
                     pl.BlockSpec((None, gates_t.shape[1], M_CHUNK), lambda b, s: (b, 0, chunk(s)))]

    of_spec, fwd_specs = specs(lambda s: s)
    or_spec, rev_specs = specs(rev_chunk)
    args = (qm, kt, vm, gates, gates_t)
    return pl.pallas_call(
        _mlstm_scan_kernel,
        grid=(n_batch, cpb),
        in_specs=fwd_specs + rev_specs,
        out_specs=[of_spec, or_spec],
        out_shape=[jax.ShapeDtypeStruct((N, M_WIDTH), f32)] * 2,
        scratch_shapes=[pltpu.VMEM((2 * M_HEADS, HEAD_DIM, 2 * HEAD_DIM), f32),
                        pltpu.VMEM((2 * M_HEADS, SUBLANES, LANES), f32)],
        compiler_params=_cparams("parallel", "arbitrary"),
        name="mlstm_scan",
    )(*args, *args)


def _merge_kernel(oa_ref, oc_ref, hf_ref, hr_ref, mo_ref, g_ref, *rest, d_model, tpb, sub, split):
    if split:
        x_ref, c_ref, mod_ref, ng_ref, wa_ref, wm_ref, wc_ref, wo_ref, o_ref = rest
        resid = _select_tokens(x_ref, c_ref, pl.program_id(0), tpb, sub)
    else:
        x_ref, mod_ref, ng_ref, wa_ref, wm_ref, wc_ref, wo_ref, o_ref = rest
        resid = x_ref[...]
    ng = ng_ref[...]
    mo = mo_ref[...]
    hsum = hf_ref[...] + hr_ref[...]
    parts = []
    for h in range(M_HEADS):
        sl = slice(h * HEAD_DIM, (h + 1) * HEAD_DIM)
        parts.append(_rms(hsum[:, sl], ng[:, sl]))
    om = (jnp.concatenate(parts, axis=-1) * jax.nn.sigmoid(mo)).astype(bf16)
    D = d_model
    ya = jnp.dot(oa_ref[...], wa_ref[...], preferred_element_type=f32)
    y = jax.nn.sigmoid(g_ref[:, 0:D]) * ya
    ym = jnp.dot(om, wm_ref[...], preferred_element_type=f32)
    y = y + jax.nn.sigmoid(g_ref[:, D:2 * D]) * ym
    yc = jnp.dot(oc_ref[...], wc_ref[...], preferred_element_type=f32)
    y = y + jax.nn.sigmoid(g_ref[:, 2 * D:3 * D]) * yc
    out = jnp.dot(y.astype(bf16), wo_ref[...], preferred_element_type=f32)
    o_ref[...] = resid + mod_ref[2:3, :] * out


def _merge(oa, oc, hf, hr, P, cols, xa, mod, norm_g, wa, wm, wc, wo, tpb, n_batch):
    split = isinstance(xa, tuple)
    N, D = P.shape[0], wo.shape[1]
    tm = ROW_TILE
    sub = ROW_TILE // tm
    mo_off, mo_w = cols["mo"]
    g_off, g_w = cols["g"]
    assert mo_off % mo_w == 0 and g_off == 0

    def rows(w):
        return pl.BlockSpec((tm, w), lambda t: (t, 0))

    def whole(a):
        return pl.BlockSpec(a.shape, lambda t: (0, 0), pipeline_mode=pl.Buffered(1))

    resid_specs = _split_token_specs(D, tpb, sub) if split else [rows(D)]
    resid_args = xa if split else (xa,)
    return pl.pallas_call(
        functools.partial(_merge_kernel, d_model=D, tpb=tpb, sub=sub, split=split),
        grid=(N // tm,),
        in_specs=[rows(A_WIDTH), rows(C_WIDTH), rows(M_WIDTH), rows(M_WIDTH),
                  pl.BlockSpec((tm, mo_w), lambda t: (t, mo_off // mo_w)),
                  pl.BlockSpec((tm, g_w), lambda t: (t, 0)),
                  *resid_specs,
                  pl.BlockSpec((None, SUBLANES, D), lambda t: (_mod_row(t // sub, tpb, n_batch), 0, 0)),
                  pl.BlockSpec((1, M_WIDTH), lambda t: (0, 0)),
                  whole(wa), whole(wm), whole(wc), whole(wo)],
        out_specs=rows(D),
        out_shape=jax.ShapeDtypeStruct((N, D), f32),
        compiler_params=_cparams("parallel"),
        name="merge",
    )(oa, oc, hf, hr, P, P, *resid_args, mod, norm_g.reshape(1, M_WIDTH), wa, wm, wc, wo)


def _router_kernel(x_ref, mod_ref, g_ref, whi_ref, wlo_ref, br_ref, route_o):
    m = mod_ref[...]
    h = _rms(x_ref[...], g_ref[...]) * (1.0 + m[4:5]) + m[3:4]
    h_hi = h.astype(bf16)
    h_lo = (h - h_hi.astype(f32)).astype(bf16)
    logits = (jnp.dot(h_hi, whi_ref[...], preferred_element_type=f32)
              + jnp.dot(h_hi, wlo_ref[...], preferred_element_type=f32)
              + jnp.dot(h_lo, whi_ref[...], preferred_element_type=f32)) + br_ref[...]
    lane = lax.broadcasted_iota(jnp.int32, logits.shape, 1)
    big = jnp.int32(LANES)
    is_g = jnp.logical_and(lane >= N_EXPERTS, lane < N_EXPERTS + N_GROUPS)
    gl = jnp.where(is_g, logits, -jnp.inf)
    gmax = jnp.max(gl, axis=-1, keepdims=True)
    g_sel = jnp.min(jnp.where(gl == gmax, lane, big), axis=-1, keepdims=True) - N_EXPERTS
    p_g = 1.0 / jnp.sum(jnp.where(is_g, jnp.exp(gl - gmax), 0.0), axis=-1, keepdims=True)
    lo = g_sel * EXPERTS_PER_GROUP
    in_grp = jnp.logical_and(lane >= lo, lane < lo + EXPERTS_PER_GROUP)
    el = jnp.where(in_grp, logits, -jnp.inf)
    e1 = jnp.max(el, axis=-1, keepdims=True)
    i1 = jnp.min(jnp.where(el == e1, lane, big), axis=-1, keepdims=True)
    el2 = jnp.where(lane == i1, -jnp.inf, el)
    e2 = jnp.max(el2, axis=-1, keepdims=True)
    i2 = jnp.min(jnp.where(el2 == e2, lane, big), axis=-1, keepdims=True)
    r = jnp.exp(e2 - e1)
    w1 = p_g / (1.0 + r)
    w2 = p_g * r / (1.0 + r)
    route_o[...] = jnp.where(lane == 0, i1.astype(f32),
                             jnp.where(lane == 1, i2.astype(f32),
                                       jnp.where(lane == 2, w1, jnp.where(lane == 3, w2, 0.0))))


def _router(x1, mod, g, w_router, b_router, tpb, n_batch):
    N, D = x1.shape
    w_hi = w_router.astype(bf16)
    w_lo = (w_router - w_hi.astype(f32)).astype(bf16)
    return pl.pallas_call(
        _router_kernel,
        grid=(N // ROW_TILE,),
        in_specs=[pl.BlockSpec((ROW_TILE, D), lambda t: (t, 0)),
                  pl.BlockSpec((None, SUBLANES, D), lambda t: (_mod_row(t, tpb, n_batch), 0, 0)),
                  pl.BlockSpec((1, D), lambda t: (0, 0)),
                  pl.BlockSpec((D, LANES), lambda t: (0, 0)),
                  pl.BlockSpec((D, LANES), lambda t: (0, 0)),
                  pl.BlockSpec((1, LANES), lambda t: (0, 0))],
        out_specs=pl.BlockSpec((ROW_TILE, LANES), lambda t: (t, 0)),
        out_shape=jax.ShapeDtypeStruct((N, LANES), f32),
        compiler_params=_cparams("parallel"),
        name="moe_router",
    )(x1, mod, g.reshape(1, D), w_hi, w_lo, b_router)


MOE_TILE = 256
DMA_UNROLL = 8


def _dispatch_plan(route):
    N = route.shape[0]
    P = 2 * N
    e_flat = route[:, 0:2].astype(jnp.int32).reshape(P)
    lanes = jnp.arange(N_EXPERTS, dtype=jnp.int32)
    onehot = (e_flat[:, None] == lanes[None, :]).astype(jnp.int32)
    csum = jnp.cumsum(onehot, axis=0)
    counts = csum[-1]
    rank = jnp.sum((csum - onehot) * onehot, axis=1)
    padded = ((counts + MOE_TILE - 1) // MOE_TILE) * MOE_TILE
    ends = jnp.cumsum(padded)
    starts = ends - padded
    slot = jnp.sum(onehot * starts[None, :], axis=1) + rank
    n_slots = P + N_EXPERTS * MOE_TILE
    n_tiles = n_slots // MOE_TILE
    tile_start = jnp.arange(n_tiles, dtype=jnp.int32) * MOE_TILE
    tile_expert = jnp.minimum(jnp.sum((tile_start[:, None] >= ends[None, :]).astype(jnp.int32), axis=1),
                              N_EXPERTS - 1)
    n_active = (ends[-1] // MOE_TILE).reshape(1)
    n_pad_e = padded - counts
    pad_off = jnp.cumsum(n_pad_e) - n_pad_e
    n_pad = jnp.sum(n_pad_e)
    q = jnp.arange(n_slots - P, dtype=jnp.int32)
    e_of_q = jnp.minimum(jnp.sum((q[:, None] >= (pad_off + n_pad_e)[None, :]).astype(jnp.int32), axis=1),
                         N_EXPERTS - 1)
    oh_q = (e_of_q[:, None] == lanes[None, :]).astype(jnp.int32)
    in_run = jnp.sum(oh_q * (starts + counts - pad_off)[None, :], axis=1) + q
    pad_slots = jnp.where(q < n_pad, in_run, ends[-1] + q - n_pad)
    slot_kmajor = slot.reshape(N, 2).T.reshape(P)
    return dict(tile_expert=tile_expert, n_active=n_active, slot_kmajor=slot_kmajor, pad_slots=pad_slots,
                n_slots=n_slots)


def _row_copy(src, src_row, dst, dst_row, sem):
    return pltpu.make_async_copy(src.at[pl.ds(src_row, 1)], dst.at[pl.ds(dst_row, 1)], sem)


def _wait_row_copies(src, dst, n_rows, sem):
    def body(r, carry):
        for u in range(DMA_UNROLL):
            _row_copy(src, 0, dst, 0, sem).wait()
        return carry
    lax.fori_loop(0, n_rows // DMA_UNROLL, body, 0)


def _moe_dispatch_kernel(slot_ref, pad_ref, x_ref, mod_ref, g_ref, xs_hbm, hbuf, zrow, sem, zsem, *, n_tok):
    i = pl.program_id(0)
    n = pl.num_programs(0)
    cur = i % 2
    tm = x_ref.shape[0]

    def wait_buf(b):
        _wait_row_copies(hbuf.at[b], xs_hbm, 2 * tm, sem.at[b])

    @pl.when(i >= 2)
    def _():
        wait_buf(cur)

    m = mod_ref[...]
    hbuf[cur] = _rms(x_ref[...], g_ref[...]) * (1.0 + m[4:5]) + m[3:4]

    for b in range(2):
        @pl.when(cur == b)
        def _(b=b):
            for rr in range(tm):
                for k in range(2):
                    _row_copy(hbuf.at[b], rr, xs_hbm, slot_ref[k * n_tok + i * tm + rr], sem.at[b]).start()

    @pl.when(i == 0)
    def _():
        zrow[...] = jnp.zeros(zrow.shape, f32)
        n_pad = pad_ref.shape[0]

        def zstart(p, carry):
            for u in range(DMA_UNROLL):
                _row_copy(zrow, 0, xs_hbm, pad_ref[p * DMA_UNROLL + u], zsem).start()
            return carry
        lax.fori_loop(0, n_pad // DMA_UNROLL, zstart, 0)
        _wait_row_copies(zrow, xs_hbm, n_pad, zsem)

    @pl.when(i == n - 1)
    def _():
        wait_buf(cur)

        @pl.when(n >= 2)
        def _():
            wait_buf(1 - cur)


def _moe_dispatch(x1, plan, mod, g, tpb, n_batch):
    N, D = x1.shape
    grid_spec = pltpu.PrefetchScalarGridSpec(
        num_scalar_prefetch=2,
        grid=(N // ROW_TILE,),
        in_specs=[pl.BlockSpec((ROW_TILE, D), lambda t, *_: (t, 0)),
                  pl.BlockSpec((None, SUBLANES, D), lambda t, *_: (_mod_row(t, tpb, n_batch), 0, 0)),
                  pl.BlockSpec((1, D), lambda t, *_: (0, 0))],
        out_specs=pl.BlockSpec(memory_space=pl.ANY),
        scratch_shapes=[pltpu.VMEM((2, ROW_TILE, D), f32), pltpu.VMEM((SUBLANES, D), f32),
                        pltpu.SemaphoreType.DMA((2,)), pltpu.SemaphoreType.DMA(())])
    return pl.pallas_call(
        functools.partial(_moe_dispatch_kernel, n_tok=N),
        grid_spec=grid_spec,
        out_shape=jax.ShapeDtypeStruct((plan["n_slots"], D), f32),
        compiler_params=_cparams("arbitrary"),
        name="moe_dispatch",
    )(plan["slot_kmajor"], plan["pad_slots"], x1, mod, g.reshape(1, D))


def _moe_experts_kernel(te_ref, nact_ref, xs_ref, wg_ref, wu_ref, wd_ref, y_ref, wg_sc, wu_sc, wd_sc):
    i = pl.program_id(0)
    n_act = nact_ref[0]
    new_expert = jnp.logical_or(i == 0, te_ref[i] != te_ref[jnp.maximum(i - 1, 0)])

    @pl.when(jnp.logical_and(new_expert, i < n_act))
    def _():
        wg_sc[...] = wg_ref[...].astype(bf16)
        wu_sc[...] = wu_ref[...].astype(bf16)
        wd_sc[...] = wd_ref[...].astype(bf16)

    @pl.when(i < n_act)
    def _():
        x = xs_ref[...].astype(bf16)
        a = jnp.dot(x, wg_sc[...], preferred_element_type=f32)
        u = jnp.dot(x, wu_sc[...], preferred_element_type=f32)
        mid = (a * jax.nn.sigmoid(a)) * u
        y_ref[...] = jnp.dot(mid.astype(bf16), wd_sc[...], preferred_element_type=f32)

    @pl.when(i >= n_act)
    def _():
        y_ref[...] = jnp.zeros(y_ref.shape, f32)


def _moe_experts(xs, plan, wg, wu, wd, layer):
    n_slots, D = xs.shape
    _, E, _, F = wg.shape

    def w_spec(shape):
        return pl.BlockSpec((None, None) + shape,
                            lambda i, te, na: (layer, te[jnp.minimum(i, na[0] - 1)], 0, 0))

    grid_spec = pltpu.PrefetchScalarGridSpec(
        num_scalar_prefetch=2,
        grid=(n_slots // MOE_TILE,),
        in_specs=[pl.BlockSpec((MOE_TILE, D), lambda i, te, na: (jnp.minimum(i, na[0] - 1), 0)),
                  w_spec((D, F)), w_spec((D, F)), w_spec((F, D))],
        out_specs=pl.BlockSpec((MOE_TILE, D), lambda i, te, na: (i, 0)),
        scratch_shapes=[pltpu.VMEM((D, F), bf16), pltpu.VMEM((D, F), bf16), pltpu.VMEM((F, D), bf16)])
    return pl.pallas_call(
        _moe_experts_kernel,
        grid_spec=grid_spec,
        out_shape=jax.ShapeDtypeStruct((n_slots, D), f32),
        compiler_params=_cparams("arbitrary"),
        name="moe_experts",
    )(plan["tile_expert"], plan["n_active"], xs, wg, wu, wd)


def _moe_combine_kernel(slot_ref, y_hbm, x_ref, route_ref, mod_ref, ng_ref, nmod_ref, *rest, n_tok, tpb, lat_only,
                        final):
    if final:
        (o_ref, ybuf, sem) = rest
    else:
        (o_ref, h_ref, ybuf, sem) = rest
    i = pl.program_id(0)
    n = pl.num_programs(0)
    cur = i % 2
    tm = x_ref.shape[0]

    def tile_row0(j):
        if lat_only:
            return ((j // (tpb - 1)) * tpb + 1 + j % (tpb - 1)) * tm
        return j * tm

    def start_rolled(j, buf):
        def body(r, carry):
            for u in range(DMA_UNROLL):
                rr = r * DMA_UNROLL + u
                for k in range(2):
                    _row_copy(y_hbm, slot_ref[k * n_tok + tile_row0(j) + rr], ybuf.at[buf, k], rr, sem.at[buf]).start()
            return carry
        lax.fori_loop(0, tm // DMA_UNROLL, body, 0)

    def start_unrolled(j, buf):
        base = tile_row0(j)
        for rr in range(tm):
            for k in range(2):
                _row_copy(y_hbm, slot_ref[k * n_tok + base + rr], ybuf.at[buf, k], rr, sem.at[buf]).start()

    @pl.when(i == 0)
    def _():
        start_rolled(0, 0)

    for b in range(2):
        @pl.when(jnp.logical_and(i + 1 < n, cur == 1 - b))
        def _(b=b):
            start_unrolled(i + 1, b)

    _wait_row_copies(y_hbm, ybuf.at[cur, 0], 2 * tm, sem.at[cur])
    route = route_ref[...]
    moe = route[:, 2:3] * ybuf[cur, 0] + route[:, 3:4] * ybuf[cur, 1]
    xo = x_ref[...] + mod_ref[5:6, :] * moe
    if final:
        o_ref[...] = _rms(xo, ng_ref[...])
    else:
        o_ref[...] = xo
        nm = nmod_ref[...]
        h_ref[...] = (_rms(xo, ng_ref[...]) * (1.0 + nm[1:2]) + nm[0:1]).astype(h_ref.dtype)


def _moe_combine(y, plan, x1, route, mod, next_g, next_mod, tpb, n_batch, final):
    N, D = x1.shape
    lat_only = final
    lt = tpb - 1

    def tok_tile(t):
        return (t // lt) * tpb + 1 + t % lt if lat_only else t

    n_steps = n_batch * lt if lat_only else N // ROW_TILE
    row = lambda w: pl.BlockSpec((ROW_TILE, w), lambda t, s: (tok_tile(t), 0))
    modspec = pl.BlockSpec((None, SUBLANES, D), lambda t, s: (_mod_row(tok_tile(t), tpb, n_batch), 0, 0))
    out_row = pl.BlockSpec((ROW_TILE, D), lambda t, s: (t, 0))
    if final:
        out_specs, out_shape = out_row, jax.ShapeDtypeStruct((n_steps * ROW_TILE, D), f32)
    else:
        out_specs = [out_row, out_row]
        out_shape = [jax.ShapeDtypeStruct((N, D), f32), jax.ShapeDtypeStruct((N, D), bf16)]
    grid_spec = pltpu.PrefetchScalarGridSpec(
        num_scalar_prefetch=1,
        grid=(n_steps,),
        in_specs=[pl.BlockSpec(memory_space=pl.ANY), row(D), row(LANES), modspec,
                  pl.BlockSpec((1, D), lambda t, s: (0, 0)), modspec],
        out_specs=out_specs,
        scratch_shapes=[pltpu.VMEM((2, 2, ROW_TILE, D), f32), pltpu.SemaphoreType.DMA((2,))])
    return pl.pallas_call(
        functools.partial(_moe_combine_kernel, n_tok=N, tpb=tpb, lat_only=lat_only, final=final),
        grid_spec=grid_spec,
        out_shape=out_shape,
        compiler_params=_cparams("arbitrary"),
        name="moe_combine_final" if final else "moe_combine",
    )(plan["slot_kmajor"], y, x1, route, mod, next_g.reshape(1, D), next_mod)


def _rope_tables(n_ctx, n_lat):
    n_rows = n_lat // GRID_W
    rows, cols = jnp.meshgrid(jnp.arange(n_rows), jnp.arange(GRID_W), indexing='ij')
    rows = rows.reshape(-1).astype(f32)
    cols = cols.reshape(-1).astype(f32)
    axis_dim = HEAD_DIM // 2
    inv_freq = ROPE_THETA ** (-jnp.arange(0, axis_dim, 2, dtype=f32) / axis_dim)
    ar, ac = rows[:, None] * inv_freq, cols[:, None] * inv_freq
    cos = jnp.concatenate([jnp.cos(ar), jnp.cos(ar), jnp.cos(ac), jnp.cos(ac)], axis=-1)
    sin = jnp.concatenate([-jnp.sin(ar), jnp.sin(ar), -jnp.sin(ac), jnp.sin(ac)], axis=-1)
    cos = jnp.concatenate([jnp.ones((n_ctx, HEAD_DIM), f32), cos], axis=0)
    sin = jnp.concatenate([jnp.zeros((n_ctx, HEAD_DIM), f32), sin], axis=0)
    return cos, sin


def _column_plan(d_model):
    names = (("g", 3 * d_model), ("aq", A_WIDTH), ("ak", A_KV_WIDTH), ("av", A_KV_WIDTH),
             ("mq", M_WIDTH), ("mk", M_WIDTH), ("mv", M_WIDTH), ("mo", M_WIDTH),
             ("cq", C_WIDTH), ("ck", C_KV_WIDTH), ("cv", C_KV_WIDTH), ("mg", LANES))
    cols, off = {}, 0
    for name, w in names:
        cols[name] = (off, w)
        off += w
    return cols, off


def _reorder_w_in(w, d_model):
    widths = (A_WIDTH, A_KV_WIDTH, A_KV_WIDTH, M_WIDTH, M_WIDTH, M_WIDTH, M_WIDTH, N_GATE_COLS,
              C_WIDTH, C_KV_WIDTH, C_KV_WIDTH, 3 * d_model)
    names = ("aq", "ak", "av", "mq", "mk", "mv", "mo", "mg", "cq", "ck", "cv", "g")
    pieces, off = {}, 0
    for name, wd in zip(names, widths):
        pieces[name] = w[:, off:off + wd]
        off += wd
    assert off == w.shape[1]
    pieces["mg"] = jnp.pad(pieces["mg"], ((0, 0), (0, LANES - N_GATE_COLS)))
    order = ("g", "aq", "ak", "av", "mq", "mk", "mv", "mo", "cq", "ck", "cv", "mg")
    return jnp.concatenate([pieces[n] for n in order], axis=1).astype(bf16)


def kernel(x, c, ctx, c_ctx, norm1_g, norm2_g, w_mod, b_mod, w_in, a_qn_g, a_kn_g, m_conv, m_ig_b, m_fg_b,
           m_norm_g, c_sink, w_br_a, w_br_m, w_br_c, w_out, w_rg, b_rg, w_re, b_re, w_gate, w_up, w_down,
           final_g):
    B, T, D = x.shape
    Tc = ctx.shape[1]
    L = w_mod.shape[0]
    S = Tc + T
    assert Tc == ROW_TILE and T % ROW_TILE == 0 and B < SUBLANES
    tpb = S // ROW_TILE
    cols, p_width = _column_plan(D)

    xa = (x.reshape(B * T, D), ctx.reshape(B * Tc, D))
    cond = jnp.zeros((SUBLANES, D), f32).at[:B].set(c).at[B].set(c_ctx)
    mod_all = _modulation(cond, w_mod, b_mod)
    mod_all = mod_all.reshape(L, SUBLANES, 6, D)
    mod_all = jnp.pad(mod_all, ((0, 0), (0, 0), (0, SUBLANES - 6), (0, 0)))
    cos_t, sin_t = _rope_tables(Tc, T)

    h1 = _norm_modulate(*xa, mod_all[0], norm1_g[0], shift_row=0, scale_row=1, tpb=tpb, n_batch=B)
    for l in range(L):
        final = l == L - 1
        mod = mod_all[l]
        P = _matmul(h1, _reorder_w_in(w_in[l], D))
        assert P.shape[1] == p_width

        qa, ka, va, qc, kc, vc = _attn_prep(P, cols, cos_t, sin_t, a_qn_g[l], a_kn_g[l], tpb)
        oa = _attn_a(qa, ka, va, B, tpb, True, Tc, T)
        oc = _attn_c(qc, kc, vc, c_sink[l], B, tpb, Tc, T)

        gate_bias = jnp.zeros((1, LANES), f32)
        gate_bias = gate_bias.at[0, :2 * M_HEADS].set(m_ig_b[l].reshape(-1))
        gate_bias = gate_bias.at[0, 2 * M_HEADS:4 * M_HEADS].set(m_fg_b[l].reshape(-1))
        qm, km, vm, gates = _mlstm_prep(P, cols, m_conv[l], gate_bias, tpb)
        gates_t = gates.reshape(B, S, LANES)[:, :, :8 * M_HEADS].transpose(0, 2, 1)
        hf, hr = _mlstm_scan(qm, km, vm, gates, gates_t, B, Tc, T)

        x1 = _merge(oa, oc, hf, hr, P, cols, xa, mod, m_norm_g[l],
                    w_br_a[l].astype(bf16), w_br_m[l].astype(bf16), w_br_c[l].astype(bf16),
                    w_out[l].astype(bf16), tpb, B)

        w_router = jnp.pad(jnp.concatenate([w_re[l], w_rg[l]], axis=1),
                           ((0, 0), (0, LANES - N_EXPERTS - N_GROUPS)))
        b_router = jnp.pad(jnp.concatenate([b_re[l], b_rg[l]]), (0, LANES - N_EXPERTS - N_GROUPS)).reshape(1, LANES)
        route = _router(x1, mod, norm2_g[l], w_router, b_router, tpb, B)
        plan = _dispatch_plan(route)
        xs = _moe_dispatch(x1, plan, mod, norm2_g[l], tpb, B)
        y = _moe_experts(xs, plan, w_gate, w_up, w_down, l)
        if final:
            out = _moe_combine(y, plan, x1, route, mod, final_g, mod, tpb, B, True)
            return out.reshape(B, T, D)
        xa, h1 = _moe_combine(y, plan, x1, route, mod, norm1_g[l + 1], mod_all[l + 1], tpb, B, False)
```

```python
import functools

import jax
import jax.numpy as jnp
from jax import lax
from jax.experimental import pallas as pl
from jax.experimental.pallas import tpu as pltpu

f32 = jnp.float32
bf16 = jnp.bfloat16

HEAD_DIM = 128
GRID_W = 64
ROPE_THETA = 10000.0
EPS = 1e-6
NEG_INF = -1e30
A_HEADS, A_KV_HEADS = 8, 2
M_HEADS, M_CHUNK = 4, 128
C_HEADS, C_KV_HEADS = 4, 2
WINDOW = 128
N_GROUPS, EXPERTS_PER_GROUP = 4, 4
N_EXPERTS = N_GROUPS * EXPERTS_PER_GROUP

A_WIDTH = A_HEADS * HEAD_DIM
A_KV_WIDTH = A_KV_HEADS * HEAD_DIM
M_WIDTH = M_HEADS * HEAD_DIM
C_WIDTH = C_HEADS * HEAD_DIM
C_KV_WIDTH = C_KV_HEADS * HEAD_DIM
N_GATE_COLS = 4 * M_HEADS

LANES = 128
SUBLANES = 8
V7X_VMEM_BYTES = 64 * 1024 * 1024
VMEM_LIMIT = 56 * 1024 * 1024

ROW_TILE = 256
ATTN_SCALE = HEAD_DIM ** -0.5


def _cparams(*sem):
    return pltpu.CompilerParams(dimension_semantics=sem, vmem_limit_bytes=VMEM_LIMIT)


def _pick_tile(n, cap, align):
    best = align
    t = align
    while t <= min(n, cap):
        if n % t == 0:
            best = t
        t += align
    assert n % best == 0
    return best


def _mod_kernel(s_ref, w_ref, b_ref, o_ref):
    s = s_ref[...]
    s = s * jax.nn.sigmoid(s)
    o_ref[...] = jnp.dot(s.astype(bf16), w_ref[...].astype(bf16), preferred_element_type=f32) + b_ref[...]


def _modulation(cond, w_mod, b_mod):
    L, D, N6 = w_mod.shape
    tn = _pick_tile(N6, 1024, LANES)
    return pl.pallas_call(
        _mod_kernel,
        grid=(L, N6 // tn),
        in_specs=[pl.BlockSpec((SUBLANES, D), lambda l, j: (0, 0)),
                  pl.BlockSpec((None, D, tn), lambda l, j: (l, 0, j)),
                  pl.BlockSpec((None, 1, tn), lambda l, j: (l, 0, j))],
        out_specs=pl.BlockSpec((None, SUBLANES, tn), lambda l, j: (l, 0, j)),
        out_shape=jax.ShapeDtypeStruct((L, SUBLANES, N6), f32),
        compiler_params=_cparams("parallel", "parallel"),
        name="modulation",
    )(cond, w_mod, b_mod.reshape(L, 1, N6))


def _rms(x, g):
    return x * lax.rsqrt(jnp.mean(x * x, axis=-1, keepdims=True) + EPS) * g


def _split_token_specs(d_model, tpb, sub=1):
    lt = tpb - 1
    rows = ROW_TILE // sub

    def lat(i):
        t = i // sub
        return ((t // tpb) * lt + jnp.maximum(t % tpb - 1, 0)) * sub + i % sub, 0

    def ctx(i):
        t = i // sub
        return (t // tpb) * sub + i % sub, 0

    return [pl.BlockSpec((rows, d_model), lat), pl.BlockSpec((rows, d_model), ctx)]


def _select_tokens(x_ref, c_ref, step, tpb, sub=1):
    return jnp.where((step // sub) % tpb == 0, c_ref[...], x_ref[...])


def _normmod_kernel(x_ref, c_ref, mod_ref, g_ref, o_ref, *, shift_row, scale_row, tpb):
    m = mod_ref[...]
    y = _rms(_select_tokens(x_ref, c_ref, pl.program_id(0), tpb), g_ref[...])
    o_ref[...] = (y * (1.0 + m[scale_row:scale_row + 1]) + m[shift_row:shift_row + 1]).astype(o_ref.dtype)


def _mod_row(t, tiles_per_batch, n_batch):
    return jnp.where(t % tiles_per_batch == 0, n_batch, t // tiles_per_batch)


def _norm_modulate(x2, ctx2, mod, g, *, shift_row, scale_row, tpb, n_batch):
    D = x2.shape[1]
    N = x2.shape[0] + ctx2.shape[0]
    return pl.pallas_call(
        functools.partial(_normmod_kernel, shift_row=shift_row, scale_row=scale_row, tpb=tpb),
        grid=(N // ROW_TILE,),
        in_specs=_split_token_specs(D, tpb) + [
            pl.BlockSpec((None, SUBLANES, D), lambda t: (_mod_row(t, tpb, n_batch), 0, 0)),
            pl.BlockSpec((1, D), lambda t: (0, 0))],
        out_specs=pl.BlockSpec((ROW_TILE, D), lambda t: (t, 0)),
        out_shape=jax.ShapeDtypeStruct((N, D), bf16),
        compiler_params=_cparams("parallel"),
        name="norm_modulate",
    )(x2, ctx2, mod, g.reshape(1, D))


def _mm_kernel(a_ref, w_ref, o_ref):
    o_ref[...] = jnp.dot(a_ref[...], w_ref[...], preferred_element_type=f32)


def _matmul(a, w):
    M, K = a.shape
    _, N = w.shape
    tm = _pick_tile(M, 1024, ROW_TILE)
    tn = _pick_tile(N, 2304, LANES)
    return pl.pallas_call(
        _mm_kernel,
        grid=(N // tn, M // tm),
        in_specs=[pl.BlockSpec((tm, K), lambda j, i: (i, 0)),
                  pl.BlockSpec((K, tn), lambda j, i: (0, j))],
        out_specs=pl.BlockSpec((tm, tn), lambda j, i: (i, j)),
        out_shape=jax.ShapeDtypeStruct((M, N), f32),
        compiler_params=_cparams("parallel", "parallel"),
        name="in_proj",
    )(a, w)


def _rope(x, cos, sin):
    lane = lax.broadcasted_iota(jnp.int32, x.shape, 1)
    swapped = jnp.where((lane % 64) < 32, pltpu.roll(x, 96, 1), pltpu.roll(x, 32, 1))
    return x * cos + swapped * sin


def _attn_prep_kernel(aq_ref, ak_ref, av_ref, cq_ref, ck_ref, cv_ref, cos_ref, sin_ref, qn_ref, kn_ref,
                      qa_o, ka_o, va_o, qc_o, kc_o, vc_o):
    cos = cos_ref[...]
    sin = sin_ref[...]
    qn = qn_ref[...]
    kn = kn_ref[...]
    for h in range(A_HEADS):
        sl = slice(h * HEAD_DIM, (h + 1) * HEAD_DIM)
        qa_o[:, sl] = (_rope(_rms(aq_ref[:, sl], qn), cos, sin) * ATTN_SCALE).astype(bf16)
    for h in range(A_KV_HEADS):
        sl = slice(h * HEAD_DIM, (h + 1) * HEAD_DIM)
        ka_o[:, sl] = _rope(_rms(ak_ref[:, sl], kn), cos, sin).astype(bf16)
    va_o[...] = av_ref[...].astype(bf16)
    for h in range(C_HEADS):
        sl = slice(h * HEAD_DIM, (h + 1) * HEAD_DIM)
        qc_o[:, sl] = _rope(cq_ref[:, sl], cos, sin).astype(bf16)
    for h in range(C_KV_HEADS):
        sl = slice(h * HEAD_DIM, (h + 1) * HEAD_DIM)
        kc_o[:, sl] = _rope(ck_ref[:, sl], cos, sin).astype(bf16)
    vc_o[...] = cv_ref[...].astype(bf16)


def _attn_prep(P, cols, cos_t, sin_t, qn, kn, tpb):
    N = P.shape[0]

    def pspec(name):
        off, w = cols[name]
        assert off % w == 0
        return pl.BlockSpec((ROW_TILE, w), lambda t, _i=off // w: (t, _i))

    def ospec(w):
        return pl.BlockSpec((ROW_TILE, w), lambda t: (t, 0))

    tab = pl.BlockSpec((ROW_TILE, HEAD_DIM), lambda t: (t % tpb, 0))
    vec = pl.BlockSpec((1, HEAD_DIM), lambda t: (0, 0))
    widths = (A_WIDTH, A_KV_WIDTH, A_KV_WIDTH, C_WIDTH, C_KV_WIDTH, C_KV_WIDTH)
    return pl.pallas_call(
        _attn_prep_kernel,
        grid=(N // ROW_TILE,),
        in_specs=[pspec("aq"), pspec("ak"), pspec("av"), pspec("cq"), pspec("ck"), pspec("cv"),
                  tab, tab, vec, vec],
        out_specs=[ospec(w) for w in widths],
        out_shape=[jax.ShapeDtypeStruct((N, w), bf16) for w in widths],
        compiler_params=_cparams("parallel"),
        name="attn_prep",
    )(P, P, P, P, P, P, cos_t, sin_t, qn.reshape(1, HEAD_DIM), kn.reshape(1, HEAD_DIM))


def _attn_a_kernel(q_ref, k_ref, v_ref, o_ref, q_sc, m_sc, l_sc, acc_sc, *, q_off, group, n_ctx, n_lat, tk):
    qi = pl.program_id(2) + q_off
    tq = q_ref.shape[0]
    for g in range(group):
        q_sc[g * tq:(g + 1) * tq, :] = q_ref[:, g * HEAD_DIM:(g + 1) * HEAD_DIM]
    m_sc[...] = jnp.full(m_sc.shape, NEG_INF, f32)
    l_sc[...] = jnp.zeros(l_sc.shape, f32)
    acc_sc[...] = jnp.zeros(acc_sc.shape, f32)

    def chunk(start, size):
        k = k_ref[pl.ds(start, size), :]
        v = v_ref[pl.ds(start, size), :]
        s = lax.dot_general(q_sc[...], k, (((1,), (1,)), ((), ())), preferred_element_type=f32)
        m_prev = m_sc[...]
        m_new = jnp.maximum(m_prev, jnp.max(s, axis=-1, keepdims=True))
        alpha = jnp.exp(m_prev - m_new)
        p = jnp.exp(s - jnp.tile(m_new, (1, size // LANES)))
        psum = p[:, 0:LANES]
        for c in range(1, size // LANES):
            psum = psum + p[:, c * LANES:(c + 1) * LANES]
        l_sc[...] = alpha * l_sc[...] + psum
        acc_sc[...] = alpha * acc_sc[...] + jnp.dot(p.astype(bf16), v, preferred_element_type=f32)
        m_sc[...] = m_new

    chunk(0, n_ctx)

    @pl.when(qi > 0)
    def _():
        def body(j, carry):
            chunk(pl.multiple_of(n_ctx + j * tk, LANES), tk)
            return carry
        lax.fori_loop(0, n_lat // tk, body, 0)

    o = acc_sc[...] / jnp.sum(l_sc[...], axis=-1, keepdims=True)
    for g in range(group):
        o_ref[:, g * HEAD_DIM:(g + 1) * HEAD_DIM] = o[g * tq:(g + 1) * tq].astype(o_ref.dtype)


def _attn_a(qa, ka, va, n_batch, tpb, need_ctx, n_ctx, n_lat):
    N = qa.shape[0]
    S = n_ctx + n_lat
    group = A_HEADS // A_KV_HEADS
    gw = group * HEAD_DIM
    rows = group * ROW_TILE
    q_off = 0 if need_ctx else 1
    tk = _pick_tile(n_lat, 1024, LANES)
    assert n_ctx % LANES == 0
    kv_spec = pl.BlockSpec((S, HEAD_DIM), lambda b, h, i: (b, h))
    q_spec = pl.BlockSpec((ROW_TILE, gw), lambda b, h, i: (b * tpb + i + q_off, h))
    return pl.pallas_call(
        functools.partial(_attn_a_kernel, q_off=q_off, group=group, n_ctx=n_ctx, n_lat=n_lat, tk=tk),
        grid=(n_batch, A_KV_HEADS, tpb - q_off),
        in_specs=[q_spec, kv_spec, kv_spec],
        out_specs=q_spec,
        out_shape=jax.ShapeDtypeStruct((N, A_WIDTH), bf16),
        scratch_shapes=[pltpu.VMEM((rows, HEAD_DIM), bf16),
                        pltpu.VMEM((rows, LANES), f32),
                        pltpu.VMEM((rows, LANES), f32),
                        pltpu.VMEM((rows, HEAD_DIM), f32)],
        compiler_params=_cparams("parallel", "parallel", "parallel"),
        name="attn_global",
    )(qa, ka, va)


def _attn_c_kernel(sink_ref, q_ref, k_ref, v_ref, o_ref, *, group, n_ctx, n_lat, span):
    kh = pl.program_id(1)
    r = pl.program_id(2)
    tq = q_ref.shape[0]
    q = q_ref[...]
    q2 = jnp.concatenate([q[:, g * HEAD_DIM:(g + 1) * HEAD_DIM] for g in range(group)], axis=0)
    lat_q0 = (r - 1) * tq
    ws = jnp.clip(lat_q0 - WINDOW, 0, n_lat - span)
    start = pl.multiple_of(n_ctx + ws, WINDOW)
    kw = k_ref[pl.ds(start, span), :]
    vw = v_ref[pl.ds(start, span), :]
    kc = k_ref[0:n_ctx, :]
    vc = v_ref[0:n_ctx, :]
    dn = (((1,), (1,)), ((), ()))
    s_loc = lax.dot_general(q2, kw, dn, preferred_element_type=f32) * ATTN_SCALE
    s_ctx = lax.dot_general(q2, kc, dn, preferred_element_type=f32) * ATTN_SCALE
    row = lax.broadcasted_iota(jnp.int32, s_loc.shape, 0)
    col = lax.broadcasted_iota(jnp.int32, s_loc.shape, 1)
    qpos = lat_q0 + row % tq
    kpos = ws + col
    valid = jnp.logical_and(jnp.abs(qpos - kpos) <= WINDOW, r > 0)
    s_loc = jnp.where(valid, s_loc, NEG_INF)
    rowc = lax.broadcasted_iota(jnp.int32, (group * tq, 1), 0)
    sink = jnp.zeros((group * tq, 1), f32)
    for g in range(group):
        sink = jnp.where(rowc // tq == g, sink_ref[kh * group + g], sink)
    m = jnp.maximum(jnp.maximum(jnp.max(s_loc, axis=-1, keepdims=True),
                                jnp.max(s_ctx, axis=-1, keepdims=True)), sink)
    p_loc = jnp.exp(s_loc - m)
    p_ctx = jnp.exp(s_ctx - m)
    den = (jnp.sum(p_loc, axis=-1, keepdims=True) + jnp.sum(p_ctx, axis=-1, keepdims=True)
           + jnp.exp(sink - m))
    o = (jnp.dot(p_loc.astype(bf16), vw, preferred_element_type=f32)
         + jnp.dot(p_ctx.astype(bf16), vc, preferred_element_type=f32)) / den
    for g in range(group):
        o_ref[:, g * HEAD_DIM:(g + 1) * HEAD_DIM] = o[g * tq:(g + 1) * tq].astype(o_ref.dtype)


def _attn_c(qc, kc, vc, sink, n_batch, tpb, n_ctx, n_lat):
    N = qc.shape[0]
    S = n_ctx + n_lat
    group = C_HEADS // C_KV_HEADS
    gw = group * HEAD_DIM
    span = ROW_TILE + 2 * WINDOW
    assert n_lat >= span and n_ctx == ROW_TILE
    kv_spec = pl.BlockSpec((S, HEAD_DIM), lambda b, h, r: (b, h))
    return pl.pallas_call(
        functools.partial(_attn_c_kernel, group=group, n_ctx=n_ctx, n_lat=n_lat, span=span),
        grid=(n_batch, C_KV_HEADS, tpb),
        in_specs=[pl.BlockSpec(memory_space=pltpu.SMEM),
                  pl.BlockSpec((ROW_TILE, gw), lambda b, h, r: (b * tpb + r, h)),
                  kv_spec, kv_spec],
        out_specs=pl.BlockSpec((ROW_TILE, gw), lambda b, h, r: (b * tpb + r, h)),
        out_shape=jax.ShapeDtypeStruct((N, C_WIDTH), bf16),
        compiler_params=_cparams("parallel", "parallel", "parallel"),
        name="attn_window",
    )(sink, qc, kc, vc)


def _chunk_scan(x, reverse, op, ident):
    n = x.shape[0]
    pos = lax.broadcasted_iota(jnp.int32, x.shape, 0) % M_CHUNK
    k = 1
    while k < M_CHUNK:
        if reverse:
            x = op(x, jnp.where(pos < M_CHUNK - k, pltpu.roll(x, n - k, 0), ident))
        else:
            x = op(x, jnp.where(pos >= k, pltpu.roll(x, k, 0), ident))
        k *= 2
    return x


def _mlstm_prep_kernel(q_ref, qp_ref, qn_ref, k_ref, kp_ref, kn_ref, v_ref, g_ref, w_ref, gb_ref,
                       qo, kto, vo, go, *, tpb):
    r = pl.program_id(0) % tpb
    has_prev = r > 1
    has_next = jnp.logical_and(r >= 1, r < tpb - 1)
    n = q_ref.shape[0]
    row = lax.broadcasted_iota(jnp.int32, (n, 1), 0)
    w = w_ref[...]

    def conv_silu(x_ref, p_ref, n_ref, c0):
        x = x_ref[...]
        hp = jnp.where(has_prev, p_ref[SUBLANES - 1:SUBLANES, :], 0.0)
        hn = jnp.where(has_next, n_ref[0:1, :], 0.0)
        xp = jnp.where(row == 0, hp, pltpu.roll(x, 1, 0))
        xn = jnp.where(row == n - 1, hn, pltpu.roll(x, n - 1, 0))
        wc = w[:, c0:c0 + M_WIDTH]
        y = xp * wc[0:1] + x * wc[1:2] + xn * wc[2:3]
        return y * jax.nn.sigmoid(y)

    qo[...] = conv_silu(q_ref, qp_ref, qn_ref, 0).astype(bf16)
    vo[...] = v_ref[...].astype(bf16)
    kf = conv_silu(k_ref, kp_ref, kn_ref, M_WIDTH) * ATTN_SCALE
    for c in range(n // M_CHUNK):
        for h in range(M_HEADS):
            blk = kf[c * M_CHUNK:(c + 1) * M_CHUNK, h * HEAD_DIM:(h + 1) * HEAD_DIM]
            r0 = (c * M_HEADS + h) * HEAD_DIM
            kto[r0:r0 + HEAD_DIM, :] = blk.T.astype(bf16)

    g = g_ref[...] + gb_ref[...]
    lane = lax.broadcasted_iota(jnp.int32, g.shape, 1)
    nh2 = 2 * M_HEADS
    lf = pltpu.roll(jnp.minimum(g, 0.0) - jnp.log1p(jnp.exp(-jnp.abs(g))), LANES - nh2, 1)
    fwd = lane < M_HEADS
    b = jnp.where(fwd, _chunk_scan(lf, False, jnp.add, 0.0), _chunk_scan(lf, True, jnp.add, 0.0))
    r = g - b
    cm = jnp.where(fwd, _chunk_scan(r, False, jnp.maximum, -jnp.inf), _chunk_scan(r, True, jnp.maximum, -jnp.inf))
    go[...] = jnp.where(lane < nh2, b,
                        jnp.where(lane < 2 * nh2, pltpu.roll(r, nh2, 1),
                                  jnp.where(lane < 3 * nh2, pltpu.roll(cm, 2 * nh2, 1), 0.0)))


def _mlstm_prep(P, cols, m_conv, gate_bias, tpb):
    N = P.shape[0]
    hpt = ROW_TILE // SUBLANES
    nhb = N // SUBLANES

    def main(name):
        off, w = cols[name]
        assert off % w == 0
        return pl.BlockSpec((ROW_TILE, w), lambda t, _i=off // w: (t, _i))

    def prev(name):
        off, w = cols[name]
        return pl.BlockSpec((SUBLANES, w), lambda t, _i=off // w: (jnp.maximum(t * hpt - 1, 0), _i))

    def nxt(name):
        off, w = cols[name]
        return pl.BlockSpec((SUBLANES, w), lambda t, _i=off // w: (jnp.minimum((t + 1) * hpt, nhb - 1), _i))

    ospec = pl.BlockSpec((ROW_TILE, M_WIDTH), lambda t: (t, 0))
    kt_rows = (ROW_TILE // M_CHUNK) * M_WIDTH
    return pl.pallas_call(
        functools.partial(_mlstm_prep_kernel, tpb=tpb),
        grid=(N // ROW_TILE,),
        in_specs=[main("mq"), prev("mq"), nxt("mq"), main("mk"), prev("mk"), nxt("mk"), main("mv"), main("mg"),
                  pl.BlockSpec(m_conv.shape, lambda t: (0, 0)),
                  pl.BlockSpec((1, LANES), lambda t: (0, 0))],
        out_specs=[ospec, pl.BlockSpec((kt_rows, M_CHUNK), lambda t: (t, 0)), ospec,
                   pl.BlockSpec((ROW_TILE, LANES), lambda t: (t, 0))],
        out_shape=[jax.ShapeDtypeStruct((N, M_WIDTH), bf16),
                   jax.ShapeDtypeStruct((N // M_CHUNK * M_WIDTH, M_CHUNK), bf16),
                   jax.ShapeDtypeStruct((N, M_WIDTH), bf16),
                   jax.ShapeDtypeStruct((N, LANES), f32)],
        compiler_params=_cparams("parallel"),
        name="mlstm_prep",
    )(P, P, P, P, P, P, P, P, m_conv, gate_bias)


def _mlstm_chunk_step(refs, o_ref, c_sc, m_sc, rev):
    q_ref, kt_ref, v_ref, gc_ref, gr_ref = refs
    L = M_CHUNK
    ri = lax.broadcasted_iota(jnp.int32, (L, L), 0)
    ci = lax.broadcasted_iota(jnp.int32, (L, L), 1)
    absorbed = (ci >= ri) if rev else (ci <= ri)
    lane = lax.broadcasted_iota(jnp.int32, (L, HEAD_DIM), 1)
    ones_col = jnp.where(lane == 0, 1.0, 0.0).astype(bf16)
    gc = gc_ref[...]
    gr = gr_ref[...]
    d = 1 if rev else 0
    nh2 = 2 * M_HEADS
    last = 0 if rev else L - 1
    for h in range(M_HEADS):
        st = d * M_HEADS + h
        b_col = gc[:, st:st + 1]
        cm_col = gc[:, 2 * nh2 + st:2 * nh2 + st + 1]
        r_row = gr[nh2 + st:nh2 + st + 1, :]
        sl = slice(h * HEAD_DIM, (h + 1) * HEAD_DIM)
        q = q_ref[:, sl]
        kt = kt_ref[h * HEAD_DIM:(h + 1) * HEAD_DIM, :]
        v_aug = jnp.concatenate([v_ref[:, sl], ones_col], axis=1)
        c_aug = c_sc[st]
        m = m_sc[st][0:1, 0:1]
        r_max = cm_col[last:last + 1, :]

        mm_col = jnp.maximum(m, cm_col)
        w_intra = (jnp.exp(jnp.where(absorbed, r_row - mm_col, -jnp.inf))
                   * jnp.dot(q, kt, preferred_element_type=f32))
        w_inter = jnp.exp(m - mm_col)
        tot = (w_inter * jnp.dot(q, c_aug.astype(bf16), preferred_element_type=f32)
               + jnp.dot(w_intra.astype(bf16), v_aug, preferred_element_type=f32))
        den = tot[:, HEAD_DIM:HEAD_DIM + 1]
        o_ref[:, sl] = tot[:, 0:HEAD_DIM] / jnp.maximum(jnp.abs(den), jnp.exp(-(b_col + mm_col)))

        m_top = jnp.maximum(m, r_max)
        kw_t = (kt.astype(f32) * jnp.exp(r_row - m_top)).astype(bf16)
        c_sc[st] = jnp.exp(m - m_top) * c_aug + jnp.dot(kw_t, v_aug, preferred_element_type=f32)
        m_sc[st] = jnp.broadcast_to(b_col[last:last + 1, :] + m_top, m_sc.shape[1:])


def _mlstm_scan_kernel(*refs):
    fwd_in, rev_in, (of_ref, or_ref, c_sc, m_sc) = refs[0:5], refs[5:10], refs[10:]

    @pl.when(pl.program_id(1) == 0)
    def _():
        c_sc[...] = jnp.zeros(c_sc.shape, f32)
        m_sc[...] = jnp.full(m_sc.shape, NEG_INF, f32)

    _mlstm_chunk_step(fwd_in, of_ref, c_sc, m_sc, False)
    _mlstm_chunk_step(rev_in, or_ref, c_sc, m_sc, True)


def _mlstm_scan(qm, kt, vm, gates, gates_t, n_batch, n_ctx, n_lat):
    N = qm.shape[0]
    ncc, nlc = n_ctx // M_CHUNK, n_lat // M_CHUNK
    cpb = ncc + nlc

    def rev_chunk(s):
        return jnp.where(s < ncc, ncc - 1 - s, 2 * ncc + nlc - 1 - s)

    def specs(chunk):
        row = pl.BlockSpec((M_CHUNK, M_WIDTH), lambda b, s: (b * cpb + chunk(s), 0))
        return row, [row,
                     pl.BlockSpec((M_WIDTH, M_CHUNK), lambda b, s: (b * cpb + chunk(s), 0)),
                     row,
                     pl.BlockSpec((M_CHUNK, LANES), lambda b, s: (b * cpb + chunk(s), 0)),
                     pl.BlockSpec((None, gates_t.shape[1], M_CHUNK), lambda b, s: (b, 0, chunk(s)))]

    of_spec, fwd_specs = specs(lambda s: s)
    or_spec, rev_specs = specs(rev_chunk)
    args = (qm, kt, vm, gates, gates_t)
    return pl.pallas_call(
        _mlstm_scan_kernel,
        grid=(n_batch, cpb),
        in_specs=fwd_specs + rev_specs,
        out_specs=[of_spec, or_spec],
        out_shape=[jax.ShapeDtypeStruct((N, M_WIDTH), f32)] * 2,
        scratch_shapes=[pltpu.VMEM((2 * M_HEADS, HEAD_DIM, 2 * HEAD_DIM), f32),
                        pltpu.VMEM((2 * M_HEADS, SUBLANES, LANES), f32)],
        compiler_params=_cparams("parallel", "arbitrary"),
        name="mlstm_scan",
    )(*args, *args)


def _merge_kernel(oa_ref, oc_ref, hf_ref, hr_ref, mo_ref, g_ref, *rest, d_model, tpb, sub, split):
    if split:
        x_ref, c_ref, mod_ref, ng_ref, wa_ref, wm_ref, wc_ref, wo_ref, o_ref = rest
        resid = _select_tokens(x_ref, c_ref, pl.program_id(0), tpb, sub)
    else:
        x_ref, mod_ref, ng_ref, wa_ref, wm_ref, wc_ref, wo_ref, o_ref = rest
        resid = x_ref[...]
    ng = ng_ref[...]
    mo = mo_ref[...]
    hsum = hf_ref[...] + hr_ref[...]
    parts = []
    for h in range(M_HEADS):
        sl = slice(h * HEAD_DIM, (h + 1) * HEAD_DIM)
        parts.append(_rms(hsum[:, sl], ng[:, sl]))
    om = (jnp.concatenate(parts, axis=-1) * jax.nn.sigmoid(mo)).astype(bf16)
    D = d_model
    ya = jnp.dot(oa_ref[...], wa_ref[...], preferred_element_type=f32)
    y = jax.nn.sigmoid(g_ref[:, 0:D]) * ya
    ym = jnp.dot(om, wm_ref[...], preferred_element_type=f32)
    y = y + jax.nn.sigmoid(g_ref[:, D:2 * D]) * ym
    yc = jnp.dot(oc_ref[...], wc_ref[...], preferred_element_type=f32)
    y = y + jax.nn.sigmoid(g_ref[:, 2 * D:3 * D]) * yc
    out = jnp.dot(y.astype(bf16), wo_ref[...], preferred_element_type=f32)
    o_ref[...] = resid + mod_ref[2:3, :] * out


def _merge(oa, oc, hf, hr, P, cols, xa, mod, norm_g, wa, wm, wc, wo, tpb, n_batch):
    split = isinstance(xa, tuple)
    N, D = P.shape[0], wo.shape[1]
    tm = ROW_TILE
    sub = ROW_TILE // tm
    mo_off, mo_w = cols["mo"]
    g_off, g_w = cols["g"]
    assert mo_off % mo_w == 0 and g_off == 0

    def rows(w):
        return pl.BlockSpec((tm, w), lambda t: (t, 0))

    def whole(a):
        return pl.BlockSpec(a.shape, lambda t: (0, 0), pipeline_mode=pl.Buffered(1))

    resid_specs = _split_token_specs(D, tpb, sub) if split else [rows(D)]
    resid_args = xa if split else (xa,)
    return pl.pallas_call(
        functools.partial(_merge_kernel, d_model=D, tpb=tpb, sub=sub, split=split),
        grid=(N // tm,),
        in_specs=[rows(A_WIDTH), rows(C_WIDTH), rows(M_WIDTH), rows(M_WIDTH),
                  pl.BlockSpec((tm, mo_w), lambda t: (t, mo_off // mo_w)),
                  pl.BlockSpec((tm, g_w), lambda t: (t, 0)),
                  *resid_specs,
                  pl.BlockSpec((None, SUBLANES, D), lambda t: (_mod_row(t // sub, tpb, n_batch), 0, 0)),
                  pl.BlockSpec((1, M_WIDTH), lambda t: (0, 0)),
                  whole(wa), whole(wm), whole(wc), whole(wo)],
        out_specs=rows(D),
        out_shape=jax.ShapeDtypeStruct((N, D), f32),
        compiler_params=_cparams("parallel"),
        name="merge",
    )(oa, oc, hf, hr, P, P, *resid_args, mod, norm_g.reshape(1, M_WIDTH), wa, wm, wc, wo)


def _router_kernel(x_ref, mod_ref, g_ref, whi_ref, wlo_ref, br_ref, route_o):
    m = mod_ref[...]
    h = _rms(x_ref[...], g_ref[...]) * (1.0 + m[4:5]) + m[3:4]
    h_hi = h.astype(bf16)
    h_lo = (h - h_hi.astype(f32)).astype(bf16)
    logits = (jnp.dot(h_hi, whi_ref[...], preferred_element_type=f32)
              + jnp.dot(h_hi, wlo_ref[...], preferred_element_type=f32)
              + jnp.dot(h_lo, whi_ref[...], preferred_element_type=f32)) + br_ref[...]
    lane = lax.broadcasted_iota(jnp.int32, logits.shape, 1)
    big = jnp.int32(LANES)
    is_g = jnp.logical_and(lane >= N_EXPERTS, lane < N_EXPERTS + N_GROUPS)
    gl = jnp.where(is_g, logits, -jnp.inf)
    gmax = jnp.max(gl, axis=-1, keepdims=True)
    g_sel = jnp.min(jnp.where(gl == gmax, lane, big), axis=-1, keepdims=True) - N_EXPERTS
    p_g = 1.0 / jnp.sum(jnp.where(is_g, jnp.exp(gl - gmax), 0.0), axis=-1, keepdims=True)
    lo = g_sel * EXPERTS_PER_GROUP
    in_grp = jnp.logical_and(lane >= lo, lane < lo + EXPERTS_PER_GROUP)
    el = jnp.where(in_grp, logits, -jnp.inf)
    e1 = jnp.max(el, axis=-1, keepdims=True)
    i1 = jnp.min(jnp.where(el == e1, lane, big), axis=-1, keepdims=True)
    el2 = jnp.where(lane == i1, -jnp.inf, el)
    e2 = jnp.max(el2, axis=-1, keepdims=True)
    i2 = jnp.min(jnp.where(el2 == e2, lane, big), axis=-1, keepdims=True)
    r = jnp.exp(e2 - e1)
    w1 = p_g / (1.0 + r)
    w2 = p_g * r / (1.0 + r)
    route_o[...] = jnp.where(lane == 0, i1.astype(f32),
                             jnp.where(lane == 1, i2.astype(f32),
                                       jnp.where(lane == 2, w1, jnp.where(lane == 3, w2, 0.0))))


def _router(x1, mod, g, w_router, b_router, tpb, n_batch):
    N, D = x1.shape
    w_hi = w_router.astype(bf16)
    w_lo = (w_router - w_hi.astype(f32)).astype(bf16)
    return pl.pallas_call(
        _router_kernel,
        grid=(N // ROW_TILE,),
        in_specs=[pl.BlockSpec((ROW_TILE, D), lambda t: (t, 0)),
                  pl.BlockSpec((None, SUBLANES, D), lambda t: (_mod_row(t, tpb, n_batch), 0, 0)),
                  pl.BlockSpec((1, D), lambda t: (0, 0)),
                  pl.BlockSpec((D, LANES), lambda t: (0, 0)),
                  pl.BlockSpec((D, LANES), lambda t: (0, 0)),
                  pl.BlockSpec((1, LANES), lambda t: (0, 0))],
        out_specs=pl.BlockSpec((ROW_TILE, LANES), lambda t: (t, 0)),
        out_shape=jax.ShapeDtypeStruct((N, LANES), f32),
        compiler_params=_cparams("parallel"),
        name="moe_router",
    )(x1, mod, g.reshape(1, D), w_hi, w_lo, b_router)


MOE_TILE = 256
DMA_UNROLL = 8


def _dispatch_plan(route):
    N = route.shape[0]
    P = 2 * N
    e_flat = route[:, 0:2].astype(jnp.int32).reshape(P)
    lanes = jnp.arange(N_EXPERTS, dtype=jnp.int32)
    onehot = (e_flat[:, None] == lanes[None, :]).astype(jnp.int32)
    csum = jnp.cumsum(onehot, axis=0)
    counts = csum[-1]
    rank = jnp.sum((csum - onehot) * onehot, axis=1)
    padded = ((counts + MOE_TILE - 1) // MOE_TILE) * MOE_TILE
    ends = jnp.cumsum(padded)
    starts = ends - padded
    slot = jnp.sum(onehot * starts[None, :], axis=1) + rank
    n_slots = P + N_EXPERTS * MOE_TILE
    n_tiles = n_slots // MOE_TILE
    tile_start = jnp.arange(n_tiles, dtype=jnp.int32) * MOE_TILE
    tile_expert = jnp.minimum(jnp.sum((tile_start[:, None] >= ends[None, :]).astype(jnp.int32), axis=1),
                              N_EXPERTS - 1)
    n_active = (ends[-1] // MOE_TILE).reshape(1)
    n_pad_e = padded - counts
    pad_off = jnp.cumsum(n_pad_e) - n_pad_e
    n_pad = jnp.sum(n_pad_e)
    q = jnp.arange(n_slots - P, dtype=jnp.int32)
    e_of_q = jnp.minimum(jnp.sum((q[:, None] >= (pad_off + n_pad_e)[None, :]).astype(jnp.int32), axis=1),
                         N_EXPERTS - 1)
    oh_q = (e_of_q[:, None] == lanes[None, :]).astype(jnp.int32)
    in_run = jnp.sum(oh_q * (starts + counts - pad_off)[None, :], axis=1) + q
    pad_slots = jnp.where(q < n_pad, in_run, ends[-1] + q - n_pad)
    slot_kmajor = slot.reshape(N, 2).T.reshape(P)
    return dict(tile_expert=tile_expert, n_active=n_active, slot_kmajor=slot_kmajor, pad_slots=pad_slots,
                n_slots=n_slots)


def _row_copy(src, src_row, dst, dst_row, sem):
    return pltpu.make_async_copy(src.at[pl.ds(src_row, 1)], dst.at[pl.ds(dst_row, 1)], sem)


def _wait_row_copies(src, dst, n_rows, sem):
    def body(r, carry):
        for u in range(DMA_UNROLL):
            _row_copy(src, 0, dst, 0, sem).wait()
        return carry
    lax.fori_loop(0, n_rows // DMA_UNROLL, body, 0)


def _moe_dispatch_kernel(slot_ref, pad_ref, x_ref, mod_ref, g_ref, xs_hbm, hbuf, zrow, sem, zsem, *, n_tok):
    i = pl.program_id(0)
    n = pl.num_programs(0)
    cur = i % 2
    tm = x_ref.shape[0]

    def wait_buf(b):
        _wait_row_copies(hbuf.at[b], xs_hbm, 2 * tm, sem.at[b])

    @pl.when(i >= 2)
    def _():
        wait_buf(cur)

    m = mod_ref[...]
    hbuf[cur] = _rms(x_ref[...], g_ref[...]) * (1.0 + m[4:5]) + m[3:4]

    for b in range(2):
        @pl.when(cur == b)
        def _(b=b):
            for rr in range(tm):
                for k in range(2):
                    _row_copy(hbuf.at[b], rr, xs_hbm, slot_ref[k * n_tok + i * tm + rr], sem.at[b]).start(priority=k)

    @pl.when(i == 0)
    def _():
        zrow[...] = jnp.zeros(zrow.shape, f32)
        n_pad = pad_ref.shape[0]

        def zstart(p, carry):
            for u in range(DMA_UNROLL):
                _row_copy(zrow, 0, xs_hbm, pad_ref[p * DMA_UNROLL + u], zsem).start()
            return carry
        lax.fori_loop(0, n_pad // DMA_UNROLL, zstart, 0)
        _wait_row_copies(zrow, xs_hbm, n_pad, zsem)

    @pl.when(i == n - 1)
    def _():
        wait_buf(cur)

        @pl.when(n >= 2)
        def _():
            wait_buf(1 - cur)


def _moe_dispatch(x1, plan, mod, g, tpb, n_batch):
    N, D = x1.shape
    grid_spec = pltpu.PrefetchScalarGridSpec(
        num_scalar_prefetch=2,
        grid=(N // ROW_TILE,),
        in_specs=[pl.BlockSpec((ROW_TILE, D), lambda t, *_: (t, 0)),
                  pl.BlockSpec((None, SUBLANES, D), lambda t, *_: (_mod_row(t, tpb, n_batch), 0, 0)),
                  pl.BlockSpec((1, D), lambda t, *_: (0, 0))],
        out_specs=pl.BlockSpec(memory_space=pl.ANY),
        scratch_shapes=[pltpu.VMEM((2, ROW_TILE, D), f32), pltpu.VMEM((SUBLANES, D), f32),
                        pltpu.SemaphoreType.DMA((2,)), pltpu.SemaphoreType.DMA(())])
    return pl.pallas_call(
        functools.partial(_moe_dispatch_kernel, n_tok=N),
        grid_spec=grid_spec,
        out_shape=jax.ShapeDtypeStruct((plan["n_slots"], D), f32),
        compiler_params=_cparams("arbitrary"),
        name="moe_dispatch",
    )(plan["slot_kmajor"], plan["pad_slots"], x1, mod, g.reshape(1, D))


def _moe_experts_kernel(te_ref, nact_ref, xs_ref, wg_ref, wu_ref, wd_ref, y_ref, wg_sc, wu_sc, wd_sc):
    i = pl.program_id(0)
    n_act = nact_ref[0]
    new_expert = jnp.logical_or(i == 0, te_ref[i] != te_ref[jnp.maximum(i - 1, 0)])

    @pl.when(jnp.logical_and(new_expert, i < n_act))
    def _():
        wg_sc[...] = wg_ref[...].astype(bf16)
        wu_sc[...] = wu_ref[...].astype(bf16)
        wd_sc[...] = wd_ref[...].astype(bf16)

    @pl.when(i < n_act)
    def _():
        x = xs_ref[...].astype(bf16)
        a = jnp.dot(x, wg_sc[...], preferred_element_type=f32)
        u = jnp.dot(x, wu_sc[...], preferred_element_type=f32)
        mid = (a * jax.nn.sigmoid(a)) * u
        y_ref[...] = jnp.dot(mid.astype(bf16), wd_sc[...], preferred_element_type=f32)

    @pl.when(i >= n_act)
    def _():
        y_ref[...] = jnp.zeros(y_ref.shape, f32)


def _moe_experts(xs, plan, wg, wu, wd, layer):
    n_slots, D = xs.shape
    _, E, _, F = wg.shape

    def w_spec(shape):
        return pl.BlockSpec((None, None) + shape,
                            lambda i, te, na: (layer, te[jnp.minimum(i, na[0] - 1)], 0, 0))

    grid_spec = pltpu.PrefetchScalarGridSpec(
        num_scalar_prefetch=2,
        grid=(n_slots // MOE_TILE,),
        in_specs=[pl.BlockSpec((MOE_TILE, D), lambda i, te, na: (jnp.minimum(i, na[0] - 1), 0)),
                  w_spec((D, F)), w_spec((D, F)), w_spec((F, D))],
        out_specs=pl.BlockSpec((MOE_TILE, D), lambda i, te, na: (i, 0)),
        scratch_shapes=[pltpu.VMEM((D, F), bf16), pltpu.VMEM((D, F), bf16), pltpu.VMEM((F, D), bf16)])
    return pl.pallas_call(
        _moe_experts_kernel,
        grid_spec=grid_spec,
        out_shape=jax.ShapeDtypeStruct((n_slots, D), f32),
        compiler_params=_cparams("arbitrary"),
        name="moe_experts",
    )(plan["tile_expert"], plan["n_active"], xs, wg, wu, wd)


def _moe_combine_kernel(slot_ref, y_hbm, x_ref, route_ref, mod_ref, ng_ref, nmod_ref, *rest, n_tok, tpb, lat_only,
                        final):
    if final:
        (o_ref, ybuf, sem) = rest
    else:
        (o_ref, h_ref, ybuf, sem) = rest
    i = pl.program_id(0)
    n = pl.num_programs(0)
    cur = i % 2
    tm = x_ref.shape[0]

    def tile_row0(j):
        if lat_only:
            return ((j // (tpb - 1)) * tpb + 1 + j % (tpb - 1)) * tm
        return j * tm

    def start_rolled(j, buf):
        def body(r, carry):
            for u in range(DMA_UNROLL):
                rr = r * DMA_UNROLL + u
                for k in range(2):
                    _row_copy(y_hbm, slot_ref[k * n_tok + tile_row0(j) + rr], ybuf.at[buf, k], rr, sem.at[buf]).start()
            return carry
        lax.fori_loop(0, tm // DMA_UNROLL, body, 0)

    def start_unrolled(j, buf):
        base = tile_row0(j)
        for rr in range(tm):
            for k in range(2):
                _row_copy(y_hbm, slot_ref[k * n_tok + base + rr], ybuf.at[buf, k], rr, sem.at[buf]).start(priority=k)

    @pl.when(i == 0)
    def _():
        start_rolled(0, 0)

    for b in range(2):
        @pl.when(jnp.logical_and(i + 1 < n, cur == 1 - b))
        def _(b=b):
            start_unrolled(i + 1, b)

    _wait_row_copies(y_hbm, ybuf.at[cur, 0], 2 * tm, sem.at[cur])
    route = route_ref[...]
    moe = route[:, 2:3] * ybuf[cur, 0] + route[:, 3:4] * ybuf[cur, 1]
    xo = x_ref[...] + mod_ref[5:6, :] * moe
    if final:
        o_ref[...] = _rms(xo, ng_ref[...])
    else:
        o_ref[...] = xo
        nm = nmod_ref[...]
        h_ref[...] = (_rms(xo, ng_ref[...]) * (1.0 + nm[1:2]) + nm[0:1]).astype(h_ref.dtype)


def _moe_combine(y, plan, x1, route, mod, next_g, next_mod, tpb, n_batch, final):
    N, D = x1.shape
    lat_only = final
    lt = tpb - 1

    def tok_tile(t):
        return (t // lt) * tpb + 1 + t % lt if lat_only else t

    n_steps = n_batch * lt if lat_only else N // ROW_TILE
    row = lambda w: pl.BlockSpec((ROW_TILE, w), lambda t, s: (tok_tile(t), 0))
    modspec = pl.BlockSpec((None, SUBLANES, D), lambda t, s: (_mod_row(tok_tile(t), tpb, n_batch), 0, 0))
    out_row = pl.BlockSpec((ROW_TILE, D), lambda t, s: (t, 0))
    if final:
        out_specs, out_shape = out_row, jax.ShapeDtypeStruct((n_steps * ROW_TILE, D), f32)
    else:
        out_specs = [out_row, out_row]
        out_shape = [jax.ShapeDtypeStruct((N, D), f32), jax.ShapeDtypeStruct((N, D), bf16)]
    grid_spec = pltpu.PrefetchScalarGridSpec(
        num_scalar_prefetch=1,
        grid=(n_steps,),
        in_specs=[pl.BlockSpec(memory_space=pl.ANY), row(D), row(LANES), modspec,
                  pl.BlockSpec((1, D), lambda t, s: (0, 0)), modspec],
        out_specs=out_specs,
        scratch_shapes=[pltpu.VMEM((2, 2, ROW_TILE, D), f32), pltpu.SemaphoreType.DMA((2,))])
    return pl.pallas_call(
        functools.partial(_moe_combine_kernel, n_tok=N, tpb=tpb, lat_only=lat_only, final=final),
        grid_spec=grid_spec,
        out_shape=out_shape,
        compiler_params=_cparams("arbitrary"),
        name="moe_combine_final" if final else "moe_combine",
    )(plan["slot_kmajor"], y, x1, route, mod, next_g.reshape(1, D), next_mod)


def _rope_tables(n_ctx, n_lat):
    n_rows = n_lat // GRID_W
    rows, cols = jnp.meshgrid(jnp.arange(n_rows), jnp.arange(GRID_W), indexing='ij')
    rows = rows.reshape(-1).astype(f32)
    cols = cols.reshape(-1).astype(f32)
    axis_dim = HEAD_DIM // 2
    inv_freq = ROPE_THETA ** (-jnp.arange(0, axis_dim, 2, dtype=f32) / axis_dim)
    ar, ac = rows[:, None] * inv_freq, cols[:, None] * inv_freq
    cos = jnp.concatenate([jnp.cos(ar), jnp.cos(ar), jnp.cos(ac), jnp.cos(ac)], axis=-1)
    sin = jnp.concatenate([-jnp.sin(ar), jnp.sin(ar), -jnp.sin(ac), jnp.sin(ac)], axis=-1)
    cos = jnp.concatenate([jnp.ones((n_ctx, HEAD_DIM), f32), cos], axis=0)
    sin = jnp.concatenate([jnp.zeros((n_ctx, HEAD_DIM), f32), sin], axis=0)
    return cos, sin


def _column_plan(d_model):
    names = (("g", 3 * d_model), ("aq", A_WIDTH), ("ak", A_KV_WIDTH), ("av", A_KV_WIDTH),
             ("mq", M_WIDTH), ("mk", M_WIDTH), ("mv", M_WIDTH), ("mo", M_WIDTH),
             ("cq", C_WIDTH), ("ck", C_KV_WIDTH), ("cv", C_KV_WIDTH), ("mg", LANES))
    cols, off = {}, 0
    for name, w in names:
        cols[name] = (off, w)
        off += w
    return cols, off


def _reorder_w_in(w, d_model):
    widths = (A_WIDTH, A_KV_WIDTH, A_KV_WIDTH, M_WIDTH, M_WIDTH, M_WIDTH, M_WIDTH, N_GATE_COLS,
              C_WIDTH, C_KV_WIDTH, C_KV_WIDTH, 3 * d_model)
    names = ("aq", "ak", "av", "mq", "mk", "mv", "mo", "mg", "cq", "ck", "cv", "g")
    pieces, off = {}, 0
    for name, wd in zip(names, widths):
        pieces[name] = w[:, off:off + wd]
        off += wd
    assert off == w.shape[1]
    pieces["mg"] = jnp.pad(pieces["mg"], ((0, 0), (0, LANES - N_GATE_COLS)))
    order = ("g", "aq", "ak", "av", "mq", "mk", "mv", "mo", "cq", "ck", "cv", "mg")
    return jnp.concatenate([pieces[n] for n in order], axis=1).astype(bf16)


def kernel(x, c, ctx, c_ctx, norm1_g, norm2_g, w_mod, b_mod, w_in, a_qn_g, a_kn_g, m_conv, m_ig_b, m_fg_b,
           m_norm_g, c_sink, w_br_a, w_br_m, w_br_c, w_out, w_rg, b_rg, w_re, b_re, w_gate, w_up, w_down,
           final_g):
    B, T, D = x.shape
    Tc = ctx.shape[1]
    L = w_mod.shape[0]
    S = Tc + T
    assert Tc == ROW_TILE and T % ROW_TILE == 0 and B < SUBLANES
    tpb = S // ROW_TILE
    cols, p_width = _column_plan(D)

    xa = (x.reshape(B * T, D), ctx.reshape(B * Tc, D))
    cond = jnp.zeros((SUBLANES, D), f32).at[:B].set(c).at[B].set(c_ctx)
    mod_all = _modulation(cond, w_mod, b_mod)
    mod_all = mod_all.reshape(L, SUBLANES, 6, D)
    mod_all = jnp.pad(mod_all, ((0, 0), (0, 0), (0, SUBLANES - 6), (0, 0)))
    cos_t, sin_t = _rope_tables(Tc, T)

    h1 = _norm_modulate(*xa, mod_all[0], norm1_g[0], shift_row=0, scale_row=1, tpb=tpb, n_batch=B)
    for l in range(L):
        final = l == L - 1
        mod = mod_all[l]
        P = _matmul(h1, _reorder_w_in(w_in[l], D))
        assert P.shape[1] == p_width

        qa, ka, va, qc, kc, vc = _attn_prep(P, cols, cos_t, sin_t, a_qn_g[l], a_kn_g[l], tpb)
        oa = _attn_a(qa, ka, va, B, tpb, True, Tc, T)
        oc = _attn_c(qc, kc, vc, c_sink[l], B, tpb, Tc, T)

        gate_bias = jnp.zeros((1, LANES), f32)
        gate_bias = gate_bias.at[0, :2 * M_HEADS].set(m_ig_b[l].reshape(-1))
        gate_bias = gate_bias.at[0, 2 * M_HEADS:4 * M_HEADS].set(m_fg_b[l].reshape(-1))
        qm, km, vm, gates = _mlstm_prep(P, cols, m_conv[l], gate_bias, tpb)
        gates_t = gates.reshape(B, S, LANES)[:, :, :8 * M_HEADS].transpose(0, 2, 1)
        hf, hr = _mlstm_scan(qm, km, vm, gates, gates_t, B, Tc, T)

        x1 = _merge(oa, oc, hf, hr, P, cols, xa, mod, m_norm_g[l],
                    w_br_a[l].astype(bf16), w_br_m[l].astype(bf16), w_br_c[l].astype(bf16),
                    w_out[l].astype(bf16), tpb, B)

        w_router = jnp.pad(jnp.concatenate([w_re[l], w_rg[l]], axis=1),
                           ((0, 0), (0, LANES - N_EXPERTS - N_GROUPS)))
        b_router = jnp.pad(jnp.concatenate([b_re[l], b_rg[l]]), (0, LANES - N_EXPERTS - N_GROUPS)).reshape(1, LANES)
        route = _router(x1, mod, norm2_g[l], w_router, b_router, tpb, B)
        plan = _dispatch_plan(route)
        xs = _moe_dispatch(x1, plan, mod, norm2_g[l], tpb, B)
        y = _moe_experts(xs, plan, w_gate, w_up, w_down, l)
        if final:
            out = _moe_combine(y, plan, x1, route, mod, final_g, mod, tpb, B, True)
            return out.reshape(B, T, D)
        xa, h1 = _moe_combine(y, plan, x1, route, mod, norm1_g[l + 1], mod_all[l + 1], tpb, B, False)
```
